```python
import math
import jax
import jax.numpy as jnp
from jax import lax
import numpy as np

D_MODEL = 1024
BATCH = 4
SEQ = 8192
DEPTH = 2

BLOCK = 128
N_BRANCH = 4
BRANCH_WIDTH = D_MODEL // 2

SWA_HEAD_DIM = 64
SWA_HEADS = BRANCH_WIDTH // SWA_HEAD_DIM
SWA_KV_HEADS = SWA_HEADS // 4
SWA_WINDOW = 128

SB_HEAD_DIM = 64
SB_HEADS = BRANCH_WIDTH // SB_HEAD_DIM

LRU_WIDTH = BRANCH_WIDTH
LRU_BLOCKS = 8
CONV_WIDTH = 4
LRU_C = 8.0

MEM_LEN = 256
MEM_HEADS = 4
MEM_HEAD_DIM = BRANCH_WIDTH // MEM_HEADS

EPS = 1e-6

SPLIT_SIZES = (
    SWA_HEADS * SWA_HEAD_DIM, SWA_KV_HEADS * SWA_HEAD_DIM, SWA_KV_HEADS * SWA_HEAD_DIM, BRANCH_WIDTH,
    BRANCH_WIDTH, BRANCH_WIDTH, BRANCH_WIDTH, BRANCH_WIDTH,
    LRU_WIDTH, LRU_WIDTH,
    BRANCH_WIDTH, BRANCH_WIDTH,
    N_BRANCH * D_MODEL,
)
IN_WIDTH = sum(SPLIT_SIZES)

kernel_name = "hybrid_gated_swa_stickbreak_rglru_mem"


def rms_norm(x, gain):
    xf = x.astype(jnp.float32)
    y = xf * lax.rsqrt(jnp.mean(xf * xf, axis=-1, keepdims=True) + EPS)
    return (y * gain.astype(jnp.float32)).astype(x.dtype)


def sliding_window_attention(q, k, v, sinks):
    b, s, h, d = q.shape
    kvh = k.shape[2]
    g = h // kvh
    nb = s // BLOCK
    qb = q.reshape(b, nb, BLOCK, kvh, g, d)

    def with_prev(t):
        tb = t.reshape(b, nb, BLOCK, kvh, d)
        prev = jnp.pad(tb[:, :-1], ((0, 0), (1, 0), (0, 0), (0, 0), (0, 0)))
        return jnp.concatenate([prev, tb], axis=2)

    kb, vb = with_prev(k), with_prev(v)
    scores = jnp.einsum('bnqkgd,bnskd->bkgnqs', qb, kb).astype(jnp.float32) * (d ** -0.5)
    qpos = jnp.arange(BLOCK)[:, None]
    kpos = jnp.arange(2 * BLOCK)[None, :] - BLOCK
    diff = qpos - kpos
    band = (diff >= 0) & (diff < SWA_WINDOW)
    key_abs = jnp.arange(nb)[:, None, None] * BLOCK + kpos[None]
    valid = band[None] & (key_abs >= 0)
    scores = jnp.where(valid, scores, -jnp.inf)
    sink = sinks.astype(jnp.float32).reshape(kvh, g)[None, :, :, None, None, None]
    m = jnp.maximum(jnp.max(scores, axis=-1, keepdims=True), sink)
    p = jnp.exp(scores - m)
    probs = p / (jnp.sum(p, axis=-1, keepdims=True) + jnp.exp(sink - m))
    out = jnp.einsum('bkgnqs,bnskd->bnqkgd', probs.astype(v.dtype), vb)
    return out.reshape(b, s, h, d)


def stick_breaking_attention(q, k, v):
    b, s, h, d = q.shape
    nb = s // BLOCK
    qb = q.reshape(b, nb, BLOCK, h, d).transpose(1, 0, 3, 2, 4)
    kpos = jnp.arange(s)
    scale = d ** -0.5

    def block(args):
        q_blk, n = args
        z = jnp.einsum('bhqd,bshd->bhqs', q_blk, k).astype(jnp.float32) * scale
        qpos = n * BLOCK + jnp.arange(BLOCK)
        causal = kpos[None, :] < qpos[:, None]
        log_keep = jnp.where(causal, jax.nn.log_sigmoid(-z), 0.0)
        between = lax.cumsum(log_keep, axis=3, reverse=True) - log_keep
        weights = jnp.where(causal, jnp.exp(jax.nn.log_sigmoid(z) + between), 0.0)
        return jnp.einsum('bhqs,bshd->bqhd', weights.astype(v.dtype), v)

    out = lax.map(block, (qb, jnp.arange(nb)))
    return out.transpose(1, 0, 2, 3, 4).reshape(b, s, h, d)


def causal_depthwise_conv(x, w, bias):
    c = x.shape[-1]
    y = lax.conv_general_dilated(x, w[:, None, :].astype(x.dtype), window_strides=(1,),
                                 padding=[(CONV_WIDTH - 1, 0)],
                                 dimension_numbers=('NWC', 'WIO', 'NWC'),
                                 feature_group_count=c)
    return y + bias


def rg_lru(x, w_a, b_a, w_x, b_x, lam):
    b, s, c = x.shape
    xb = x.reshape(b, s, LRU_BLOCKS, c // LRU_BLOCKS)
    r = jax.nn.sigmoid(jnp.einsum('bsnc,ncd->bsnd', xb, w_a).reshape(b, s, c) + b_a)
    i = jax.nn.sigmoid(jnp.einsum('bsnc,ncd->bsnd', xb, w_x).reshape(b, s, c) + b_x)
    log_a = -LRU_C * r.astype(jnp.float32) * jax.nn.softplus(-lam.astype(jnp.float32))
    a = jnp.exp(log_a)
    inp = jnp.sqrt(-jnp.expm1(2.0 * log_a)) * (i * x).astype(jnp.float32)

    def combine(left, right):
        a1, b1 = left
        a2, b2 = right
        return a1 * a2, a2 * b1 + b2

    _, h = lax.associative_scan(combine, (a, inp), axis=1)
    return h.astype(x.dtype)


def memory_attention(q, mk, mv):
    d = q.shape[-1]
    scores = jnp.einsum('bshd,bmhd->bhsm', q, mk).astype(jnp.float32) * (d ** -0.5)
    p = jax.nn.softmax(scores, axis=-1)
    return jnp.einsum('bhsm,bmhd->bshd', p.astype(mv.dtype), mv)


def hybrid_layer(x, mem, norm_gain, w_in, swa_q_gain, swa_k_gain, swa_sinks, conv_w, conv_b,
                 lru_w_a, lru_b_a, lru_w_x, lru_b_x, lru_lambda, mem_norm_gain, w_mem_kv,
                 mem_q_gain, mem_k_gain, w_branch, w_out):
    b, s, _ = x.shape
    u = rms_norm(x, norm_gain)
    proj = u @ w_in
    split_points = [int(p) for p in np.cumsum(SPLIT_SIZES)[:-1]]
    (a_q, a_k, a_v, a_g, b_q, b_k, b_v, b_g, c_x, c_g, m_q, m_g, merge) = jnp.split(proj, split_points, axis=-1)

    qa = rms_norm(a_q.reshape(b, s, SWA_HEADS, SWA_HEAD_DIM), swa_q_gain)
    ka = rms_norm(a_k.reshape(b, s, SWA_KV_HEADS, SWA_HEAD_DIM), swa_k_gain)
    va = a_v.reshape(b, s, SWA_KV_HEADS, SWA_HEAD_DIM)
    ya = sliding_window_attention(qa, ka, va, swa_sinks).reshape(b, s, BRANCH_WIDTH) * jax.nn.silu(a_g)

    yb = stick_breaking_attention(b_q.reshape(b, s, SB_HEADS, SB_HEAD_DIM),
                                  b_k.reshape(b, s, SB_HEADS, SB_HEAD_DIM),
                                  b_v.reshape(b, s, SB_HEADS, SB_HEAD_DIM)).reshape(b, s, BRANCH_WIDTH)
    yb = yb * jax.nn.silu(b_g)

    xc = causal_depthwise_conv(c_x, conv_w, conv_b)
    yc = rg_lru(xc, lru_w_a, lru_b_a, lru_w_x, lru_b_x, lru_lambda) * jax.nn.silu(c_g)

    mlen = mem.shape[1]
    mkv = rms_norm(mem, mem_norm_gain) @ w_mem_kv
    mk, mv = jnp.split(mkv, 2, axis=-1)
    mk = rms_norm(mk.reshape(b, mlen, MEM_HEADS, MEM_HEAD_DIM), mem_k_gain)
    mv = mv.reshape(b, mlen, MEM_HEADS, MEM_HEAD_DIM)
    qm = rms_norm(m_q.reshape(b, s, MEM_HEADS, MEM_HEAD_DIM), mem_q_gain)
    ym = memory_attention(qm, mk, mv).reshape(b, s, BRANCH_WIDTH) * jax.nn.silu(m_g)

    branches = jnp.stack([ya, yb, yc, ym], axis=2)
    up = jnp.einsum('bsnw,nwd->bsnd', branches, w_branch)
    gates = jax.nn.sigmoid(merge.reshape(b, s, N_BRANCH, D_MODEL))
    mixed = jnp.sum(gates * up, axis=2)
    return x + mixed @ w_out


def setup_inputs(seed: int = 0) -> dict:
    key = jax.random.key(seed)
    ks = jax.random.split(key, 24)
    f32 = jnp.float32
    hb = LRU_WIDTH // LRU_BLOCKS
    nrm = lambda k, shape, scale: jax.random.normal(k, shape, f32) * scale
    u = jax.random.uniform(ks[12], (DEPTH, LRU_WIDTH), f32, 0.9, 0.999)
    sig = u ** (1.0 / LRU_C)
    lru_lambda = jnp.log(sig) - jnp.log1p(-sig)
    return {
        'x': nrm(ks[0], (BATCH, SEQ, D_MODEL), 1.0),
        'mem': nrm(ks[1], (BATCH, MEM_LEN, D_MODEL), 1.0),
        'norm_gain': 1.0 + nrm(ks[2], (DEPTH, D_MODEL), 0.02),
        'w_in': nrm(ks[3], (DEPTH, D_MODEL, IN_WIDTH), D_MODEL ** -0.5),
        'swa_q_gain': 1.0 + nrm(ks[4], (DEPTH, SWA_HEAD_DIM), 0.02),
        'swa_k_gain': 1.0 + nrm(ks[5], (DEPTH, SWA_HEAD_DIM), 0.02),
        'swa_sinks': nrm(ks[6], (DEPTH, SWA_HEADS), 0.5),
        'conv_w': nrm(ks[7], (DEPTH, CONV_WIDTH, LRU_WIDTH), CONV_WIDTH ** -0.5),
        'conv_b': nrm(ks[8], (DEPTH, LRU_WIDTH), 0.01),
        'lru_w_a': nrm(ks[9], (DEPTH, LRU_BLOCKS, hb, hb), hb ** -0.5),
        'lru_b_a': nrm(ks[10], (DEPTH, LRU_WIDTH), 0.01),
        'lru_w_x': nrm(ks[11], (DEPTH, LRU_BLOCKS, hb, hb), hb ** -0.5),
        'lru_b_x': nrm(ks[13], (DEPTH, LRU_WIDTH), 0.01),
        'lru_lambda': lru_lambda,
        'mem_norm_gain': 1.0 + nrm(ks[14], (DEPTH, D_MODEL), 0.02),
        'w_mem_kv': nrm(ks[15], (DEPTH, D_MODEL, 2 * BRANCH_WIDTH), D_MODEL ** -0.5),
        'mem_q_gain': 1.0 + nrm(ks[16], (DEPTH, MEM_HEAD_DIM), 0.02),
        'mem_k_gain': 1.0 + nrm(ks[17], (DEPTH, MEM_HEAD_DIM), 0.02),
        'w_branch': nrm(ks[18], (DEPTH, N_BRANCH, BRANCH_WIDTH, D_MODEL), BRANCH_WIDTH ** -0.5),
        'w_out': nrm(ks[19], (DEPTH, D_MODEL, D_MODEL), D_MODEL ** -0.5),
    }


def reference(x, mem, norm_gain, w_in, swa_q_gain, swa_k_gain, swa_sinks, conv_w, conv_b,
              lru_w_a, lru_b_a, lru_w_x, lru_b_x, lru_lambda, mem_norm_gain, w_mem_kv,
              mem_q_gain, mem_k_gain, w_branch, w_out):
    for l in range(DEPTH):
        x = hybrid_layer(x, mem, norm_gain[l], w_in[l], swa_q_gain[l], swa_k_gain[l], swa_sinks[l],
                         conv_w[l], conv_b[l], lru_w_a[l], lru_b_a[l], lru_w_x[l], lru_b_x[l],
                         lru_lambda[l], mem_norm_gain[l], w_mem_kv[l], mem_q_gain[l], mem_k_gain[l],
                         w_branch[l], w_out[l])
    return x
```

```python
import functools

import jax
import jax.numpy as jnp
from jax import lax
from jax.experimental import pallas as pl
from jax.experimental.pallas import tpu as pltpu

F32 = jnp.float32
BF16 = jnp.bfloat16

D_MODEL = 1024
BRANCH_WIDTH = 512
N_BRANCH = 4
SWA_HEAD_DIM = 64
SWA_HEADS = 8
SWA_KV_HEADS = 2
SWA_WINDOW = 128
SB_HEAD_DIM = 64
LRU_BLOCKS = 8
CONV_WIDTH = 4
LRU_C = 8.0
MEM_HEADS = 4
MEM_HEAD_DIM = 128
EPS = 1e-6

LANES = 128
HEAD_PAIRS = BRANCH_WIDTH // LANES

_SIZES = (512, 128, 128, 512, 512, 512, 512, 512, 512, 512, 512, 512, 4096)
_OFFS = tuple(sum(_SIZES[:i]) for i in range(len(_SIZES)))
(OFF_AQ, OFF_AK, OFF_AV, OFF_AG, OFF_BQ, OFF_BK, OFF_BV, OFF_BG,
 OFF_CX, OFF_CG, OFF_MQ, OFF_MG, OFF_MERGE) = _OFFS
IN_WIDTH = sum(_SIZES)

TM_PROJ = 512
TM_TAIL = 512
TQ_SWA = 128
TQ_SB = 256
TK_SB = 256
TC_LRU = 512
TM_MEM = 512

VMEM_LIMIT = 56 * 1024 * 1024


def _dot(a, b):
    return jnp.dot(a, b, preferred_element_type=F32)


def _dot_nt(a, b):
    return lax.dot_general(a, b, (((1,), (1,)), ((), ())), preferred_element_type=F32)


def _sigmoid(x):
    return 1.0 / (1.0 + jnp.exp(-x))


def _silu(x):
    return x * _sigmoid(x)


def _inproj_kernel(x_ref, g_ref, w_ref, qa_ref, kva_ref, ga_ref, qb_ref, kb_ref, vb_ref,
                   gb_ref, xc_ref, gc_ref, qm_ref, gm_ref, mg_ref):
    x = x_ref[0]
    ms = jnp.mean(x * x, axis=-1, keepdims=True)
    u = (x * lax.rsqrt(ms + EPS) * g_ref[...]).astype(BF16)

    def proj(off, width):
        return _dot(u, w_ref[:, off:off + width])

    qa_ref[0] = proj(OFF_AQ, 512).astype(BF16)
    kva_ref[0] = proj(OFF_AK, 256).astype(BF16)
    ga_ref[0] = proj(OFF_AG, 512).astype(BF16)
    qb = (proj(OFF_BQ, 512) * (SB_HEAD_DIM ** -0.5)).astype(BF16)
    kb = proj(OFF_BK, 512).astype(BF16)
    vb = proj(OFF_BV, 512).astype(BF16)
    for p in range(HEAD_PAIRS):
        sl = slice(p * LANES, (p + 1) * LANES)
        qb_ref[0, p] = qb[:, sl]
        kb_ref[0, p] = kb[:, sl]
        vb_ref[0, p] = vb[:, sl]
    gb_ref[0] = proj(OFF_BG, 512).astype(BF16)
    xc_ref[0] = proj(OFF_CX, 512).astype(BF16)
    gc_ref[0] = proj(OFF_CG, 512).astype(BF16)
    qm_ref[0] = proj(OFF_MQ, 512).astype(BF16)
    gm_ref[0] = proj(OFF_MG, 512).astype(BF16)
    for n in range(N_BRANCH):
        mg_ref[0, :, n * D_MODEL:(n + 1) * D_MODEL] = proj(
            OFF_MERGE + n * D_MODEL, D_MODEL).astype(BF16)


def _inproj(x, gain, w_bf16):
    b, s, d = x.shape
    tm = TM_PROJ
    row = lambda width: pl.BlockSpec((1, tm, width), lambda bi, i: (bi, i, 0))
    pair = pl.BlockSpec((1, HEAD_PAIRS, tm, LANES), lambda bi, i: (bi, 0, i, 0))
    flat = lambda width: jax.ShapeDtypeStruct((b, s, width), BF16)
    paired = jax.ShapeDtypeStruct((b, HEAD_PAIRS, s, LANES), BF16)
    return pl.pallas_call(
        _inproj_kernel,
        grid=(b, s // tm),
        in_specs=[
            pl.BlockSpec((1, tm, d), lambda bi, i: (bi, i, 0)),
            pl.BlockSpec((1, d), lambda bi, i: (0, 0)),
            pl.BlockSpec((d, IN_WIDTH), lambda bi, i: (0, 0), pipeline_mode=pl.Buffered(1)),
        ],
        out_specs=[row(512), row(256), row(512), pair, pair, pair, row(512),
                   row(512), row(512), row(512), row(512), row(4096)],
        out_shape=[flat(512), flat(256), flat(512), paired, paired, paired, flat(512),
                   flat(512), flat(512), flat(512), flat(512), flat(4096)],
        compiler_params=pltpu.CompilerParams(
            dimension_semantics=("parallel", "parallel"), vmem_limit_bytes=VMEM_LIMIT),
        name="inproj",
    )(x, gain, w_bf16)


def _swa_kernel(sink_ref, q_ref, g_ref, kvc_ref, kvp_ref, qg_ref, kg_ref, bdq_ref, bdk_ref,
                o_ref):
    i = pl.program_id(1)
    tq = TQ_SWA
    q = q_ref[0].astype(F32)
    msq = _dot((q * q).astype(BF16), bdq_ref[...])
    qn = (q * lax.rsqrt(msq + EPS) * qg_ref[...]).astype(BF16)

    kvp = kvp_ref[0]
    kvc = kvc_ref[0]
    kcat = jnp.concatenate([kvp[:, :LANES], kvc[:, :LANES]], axis=0).astype(F32)
    vcat = jnp.concatenate([kvp[:, LANES:], kvc[:, LANES:]], axis=0).astype(F32)
    msk = _dot((kcat * kcat).astype(BF16), bdk_ref[...])
    kn_t = (kcat * lax.rsqrt(msk + EPS) * kg_ref[...]).T
    zeros = jnp.zeros((SWA_HEAD_DIM, 2 * tq), F32)

    def k_variant(half, kv):
        blk = kn_t[kv * SWA_HEAD_DIM:(kv + 1) * SWA_HEAD_DIM]
        parts = [blk, zeros] if half == 0 else [zeros, blk]
        return jnp.concatenate(parts, axis=0).astype(BF16)

    kt = [[k_variant(half, kv) for kv in range(SWA_KV_HEADS)] for half in range(2)]
    v_same = vcat.astype(BF16)
    v_swap = pltpu.roll(vcat, SWA_HEAD_DIM, axis=1).astype(BF16)

    r = lax.broadcasted_iota(jnp.int32, (tq, 2 * tq), 0)
    j = lax.broadcasted_iota(jnp.int32, (tq, 2 * tq), 1)
    diff = r - (j - tq)
    first = jnp.where(i > 0, 0, tq)
    valid = (diff >= 0) & (diff < SWA_WINDOW) & (j >= first)
    lane = lax.broadcasted_iota(jnp.int32, (tq, LANES), 1)

    outs = []
    for h in range(SWA_HEADS):
        p, half, kv = h // 2, h % 2, h // (SWA_HEADS // SWA_KV_HEADS)
        sc = _dot(qn[:, p * LANES:(p + 1) * LANES], kt[half][kv])
        sc = jnp.where(valid, sc, -jnp.inf)
        sink = sink_ref[0, h]
        m = jnp.maximum(jnp.max(sc, axis=-1, keepdims=True), sink)
        pe = jnp.exp(sc - m)
        den = jnp.sum(pe, axis=-1, keepdims=True) + jnp.exp(sink - m)
        vv = v_same if half == kv else v_swap
        outs.append(_dot(pe.astype(BF16), vv) * (1.0 / den))
    blocks = [jnp.where(lane < SWA_HEAD_DIM, outs[2 * p], outs[2 * p + 1])
              for p in range(HEAD_PAIRS)]
    y = jnp.concatenate(blocks, axis=1)
    o_ref[0] = (y * _silu(g_ref[0].astype(F32))).astype(BF16)


def _swa(qa, ga, kva, sinks, qg, kg, bdq, bdk):
    b, s, _ = qa.shape
    tq = TQ_SWA
    return pl.pallas_call(
        _swa_kernel,
        grid=(b, s // tq),
        in_specs=[
            pl.BlockSpec(memory_space=pltpu.SMEM),
            pl.BlockSpec((1, tq, 512), lambda bi, i: (bi, i, 0)),
            pl.BlockSpec((1, tq, 512), lambda bi, i: (bi, i, 0)),
            pl.BlockSpec((1, tq, 256), lambda bi, i: (bi, i, 0)),
            pl.BlockSpec((1, tq, 256), lambda bi, i: (bi, jnp.maximum(i - 1, 0), 0)),
            pl.BlockSpec((1, 512), lambda bi, i: (0, 0)),
            pl.BlockSpec((1, LANES), lambda bi, i: (0, 0)),
            pl.BlockSpec((512, 512), lambda bi, i: (0, 0)),
            pl.BlockSpec((LANES, LANES), lambda bi, i: (0, 0)),
        ],
        out_specs=pl.BlockSpec((1, tq, 512), lambda bi, i: (bi, i, 0)),
        out_shape=jax.ShapeDtypeStruct((b, s, 512), BF16),
        compiler_params=pltpu.CompilerParams(
            dimension_semantics=("parallel", "parallel"), vmem_limit_bytes=VMEM_LIMIT),
        name="swa",
    )(sinks, qa, ga, kva, kva, qg, kg, bdq, bdk)


def _sb_kernel(q_ref, k_ref, v_ref, g_ref, tri_ref, o_ref, acc_ref, car_ref):
    i = pl.program_id(2)
    tq, tk = TQ_SB, TK_SB
    q = q_ref[0, 0]
    lane = lax.broadcasted_iota(jnp.int32, (tq, LANES), 1)
    zero = jnp.zeros_like(q)
    q_half = [jnp.where(lane < SB_HEAD_DIM, q, zero), jnp.where(lane >= SB_HEAD_DIM, q, zero)]
    acc_ref[...] = jnp.zeros_like(acc_ref)
    car_ref[...] = jnp.zeros_like(car_ref)
    r = lax.broadcasted_iota(jnp.int32, (tq, tk), 0)
    c = lax.broadcasted_iota(jnp.int32, (tq, tk), 1)
    causal = c < r

    def tile(jt, masked):
        start = pl.multiple_of(jt * tk, tk)
        kj = k_ref[0, 0, pl.ds(start, tk), :]
        vj = v_ref[0, 0, pl.ds(start, tk), :]
        for a in range(2):
            z = _dot_nt(q_half[a], kj)
            sp = jnp.maximum(z, 0.0) + jnp.log(1.0 + jnp.exp(-jnp.abs(z)))
            lsz = z - sp
            if masked:
                sp = jnp.where(causal, sp, 0.0)
            cum = _dot(sp.astype(BF16), tri_ref[...])
            carry = car_ref[a]
            arg = lsz - cum - jnp.concatenate([carry] * (tk // LANES), axis=1)
            w = jnp.exp(arg)
            if masked:
                w = jnp.where(causal, w, 0.0)
            acc_ref[a] += _dot(w.astype(BF16), vj)
            car_ref[a] = carry + jnp.sum(sp, axis=-1, keepdims=True)

    tile(i, True)

    def body(jj, carry):
        tile(i - 1 - jj, False)
        return carry

    lax.fori_loop(0, i, body, 0)
    y = jnp.where(lane < SB_HEAD_DIM, acc_ref[0], acc_ref[1])
    o_ref[0] = (y * _silu(g_ref[0].astype(F32))).astype(BF16)


def _sb(qb, kb, vb, gb, tri):
    b, _, s, _ = qb.shape
    tq = TQ_SB
    return pl.pallas_call(
        _sb_kernel,
        grid=(b, HEAD_PAIRS, s // tq),
        in_specs=[
            pl.BlockSpec((1, 1, tq, LANES), lambda bi, p, i: (bi, p, i, 0)),
            pl.BlockSpec((1, 1, s, LANES), lambda bi, p, i: (bi, p, 0, 0)),
            pl.BlockSpec((1, 1, s, LANES), lambda bi, p, i: (bi, p, 0, 0)),
            pl.BlockSpec((1, tq, LANES), lambda bi, p, i: (bi, i, p)),
            pl.BlockSpec((TK_SB, TK_SB), lambda bi, p, i: (0, 0)),
        ],
        out_specs=pl.BlockSpec((1, tq, LANES), lambda bi, p, i: (bi, i, p)),
        out_shape=jax.ShapeDtypeStruct((b, s, BRANCH_WIDTH), BF16),
        scratch_shapes=[pltpu.VMEM((2, tq, LANES), F32), pltpu.VMEM((2, tq, LANES), F32)],
        compiler_params=pltpu.CompilerParams(
            dimension_semantics=("parallel", "parallel", "parallel"),
            vmem_limit_bytes=VMEM_LIMIT),
        name="stickbreak",
    )(qb, kb, vb, gb, tri)


def _lru_kernel(x_ref, g_ref, cw_ref, cb_ref, wa_ref, ba_ref, wx_ref, bx_ref, lam_ref, o_ref,
                xe_ref, h_ref):
    i = pl.program_id(1)
    tc = TC_LRU
    halo = 8

    @pl.when(i == 0)
    def _():
        xe_ref[0:halo, :] = jnp.zeros((halo, BRANCH_WIDTH), F32)
        h_ref[...] = jnp.zeros_like(h_ref)

    x = x_ref[0].astype(F32)
    xe_ref[halo:halo + tc, :] = x
    xc = cb_ref[...] + cw_ref[CONV_WIDTH - 1:CONV_WIDTH, :] * x
    for tap in range(CONV_WIDTH - 1):
        back = CONV_WIDTH - 1 - tap
        xc = xc + cw_ref[tap:tap + 1, :] * xe_ref[halo - back:halo - back + tc, :]
    xe_ref[0:halo, :] = x[tc - halo:, :]

    xb = xc.astype(BF16)
    rg = _sigmoid(_dot(xb, wa_ref[...]) + ba_ref[...])
    ig = _sigmoid(_dot(xb, wx_ref[...]) + bx_ref[...])
    lam = lam_ref[...]
    softplus_neg_lam = jnp.maximum(-lam, 0.0) + jnp.log1p(jnp.exp(-jnp.abs(lam)))
    a = jnp.exp(-LRU_C * rg * softplus_neg_lam)
    bv = jnp.sqrt(1.0 - a * a) * (ig * xc)

    row = lax.broadcasted_iota(jnp.int32, (tc, BRANCH_WIDTH), 0)
    shift = 1
    while shift < tc:
        keep = row >= shift
        a_sh = pltpu.roll(a, shift, axis=0)
        b_sh = pltpu.roll(bv, shift, axis=0)
        bv = jnp.where(keep, a * b_sh + bv, bv)
        a = jnp.where(keep, a * a_sh, a)
        shift *= 2
    h = a * h_ref[...] + bv
    h_ref[...] = h[tc - 1:tc, :]
    o_ref[0] = (h * _silu(g_ref[0].astype(F32))).astype(BF16)


def _lru(xc, gc, cw, cb, wa, ba, wx, bx, lam):
    b, s, w = xc.shape
    tc = TC_LRU
    vec = pl.BlockSpec((1, w), lambda bi, i: (0, 0))
    mat = pl.BlockSpec((w, w), lambda bi, i: (0, 0))
    return pl.pallas_call(
        _lru_kernel,
        grid=(b, s // tc),
        in_specs=[
            pl.BlockSpec((1, tc, w), lambda bi, i: (bi, i, 0)),
            pl.BlockSpec((1, tc, w), lambda bi, i: (bi, i, 0)),
            pl.BlockSpec((CONV_WIDTH, w), lambda bi, i: (0, 0)),
            vec, mat, vec, mat, vec, vec,
        ],
        out_specs=pl.BlockSpec((1, tc, w), lambda bi, i: (bi, i, 0)),
        out_shape=jax.ShapeDtypeStruct((b, s, w), BF16),
        scratch_shapes=[pltpu.VMEM((tc + 8, w), F32), pltpu.VMEM((1, w), F32)],
        compiler_params=pltpu.CompilerParams(
            dimension_semantics=("parallel", "arbitrary"), vmem_limit_bytes=VMEM_LIMIT),
        name="rglru",
    )(xc, gc, cw, cb, wa, ba, wx, bx, lam)


def _memkv_kernel(mem_ref, g_ref, w_ref, kg_ref, kt_ref, v_ref):
    m = mem_ref[0]
    ms = jnp.mean(m * m, axis=-1, keepdims=True)
    u = (m * lax.rsqrt(ms + EPS) * g_ref[...]).astype(BF16)
    kv = _dot(u, w_ref[...])
    for h in range(MEM_HEADS):
        kh = kv[:, h * MEM_HEAD_DIM:(h + 1) * MEM_HEAD_DIM]
        msk = jnp.mean(kh * kh, axis=-1, keepdims=True)
        kn = kh * lax.rsqrt(msk + EPS) * kg_ref[...]
        kt_ref[0, h * MEM_HEAD_DIM:(h + 1) * MEM_HEAD_DIM, :] = kn.T.astype(BF16)
    v_ref[0] = kv[:, BRANCH_WIDTH:].astype(BF16)


def _memkv(mem, gain, w_bf16, kg):
    b, mlen, d = mem.shape
    return pl.pallas_call(
        _memkv_kernel,
        grid=(b,),
        in_specs=[
            pl.BlockSpec((1, mlen, d), lambda bi: (bi, 0, 0)),
            pl.BlockSpec((1, d), lambda bi: (0, 0)),
            pl.BlockSpec((d, 2 * BRANCH_WIDTH), lambda bi: (0, 0)),
            pl.BlockSpec((1, MEM_HEAD_DIM), lambda bi: (0, 0)),
        ],
        out_specs=[pl.BlockSpec((1, BRANCH_WIDTH, mlen), lambda bi: (bi, 0, 0)),
                   pl.BlockSpec((1, mlen, BRANCH_WIDTH), lambda bi: (bi, 0, 0))],
        out_shape=[jax.ShapeDtypeStruct((b, BRANCH_WIDTH, mlen), BF16),
                   jax.ShapeDtypeStruct((b, mlen, BRANCH_WIDTH), BF16)],
        compiler_params=pltpu.CompilerParams(
            dimension_semantics=("parallel",), vmem_limit_bytes=VMEM_LIMIT),
        name="memkv",
    )(mem, gain, w_bf16, kg)


def _memattn_kernel(q_ref, g_ref, kt_ref, v_ref, qg_ref, o_ref):
    scale = MEM_HEAD_DIM ** -0.5
    outs = []
    for h in range(MEM_HEADS):
        sl = slice(h * MEM_HEAD_DIM, (h + 1) * MEM_HEAD_DIM)
        qh = q_ref[0, :, sl].astype(F32)
        ms = jnp.mean(qh * qh, axis=-1, keepdims=True)
        qn = (qh * lax.rsqrt(ms + EPS) * qg_ref[...]).astype(BF16)
        sc = _dot(qn, kt_ref[0, sl, :]) * scale
        m = jnp.max(sc, axis=-1, keepdims=True)
        pe = jnp.exp(sc - m)
        den = jnp.sum(pe, axis=-1, keepdims=True)
        outs.append(_dot(pe.astype(BF16), v_ref[0, :, sl]) * (1.0 / den))
    y = jnp.concatenate(outs, axis=1)
    o_ref[0] = (y * _silu(g_ref[0].astype(F32))).astype(BF16)


def _memattn(qm, gm, mkt, mv, qg):
    b, s, w = qm.shape
    mlen = mv.shape[1]
    tm = TM_MEM
    return pl.pallas_call(
        _memattn_kernel,
        grid=(b, s // tm),
        in_specs=[
            pl.BlockSpec((1, tm, w), lambda bi, i: (bi, i, 0)),
            pl.BlockSpec((1, tm, w), lambda bi, i: (bi, i, 0)),
            pl.BlockSpec((1, w, mlen), lambda bi, i: (bi, 0, 0)),
            pl.BlockSpec((1, mlen, w), lambda bi, i: (bi, 0, 0)),
            pl.BlockSpec((1, MEM_HEAD_DIM), lambda bi, i: (0, 0)),
        ],
        out_specs=pl.BlockSpec((1, tm, w), lambda bi, i: (bi, i, 0)),
        out_shape=jax.ShapeDtypeStruct((b, s, w), BF16),
        compiler_params=pltpu.CompilerParams(
            dimension_semantics=("parallel", "parallel"), vmem_limit_bytes=VMEM_LIMIT),
        name="memattn",
    )(qm, gm, mkt, mv, qg)


def _tail_kernel(x_ref, ya_ref, yb_ref, yc_ref, ym_ref, mg_ref, wb_ref, wo_ref, o_ref):
    mixed = None
    for n, y_ref in enumerate((ya_ref, yb_ref, yc_ref, ym_ref)):
        up = _dot(y_ref[0], wb_ref[n])
        gate = _sigmoid(mg_ref[0, :, n * D_MODEL:(n + 1) * D_MODEL].astype(F32))
        mixed = gate * up if mixed is None else mixed + gate * up
    o_ref[0] = x_ref[0] + _dot(mixed.astype(BF16), wo_ref[...])


def _tail(x, ya, yb, yc, ym, mg, wb_bf16, wo_bf16):
    b, s, d = x.shape
    tm = TM_TAIL
    br = pl.BlockSpec((1, tm, BRANCH_WIDTH), lambda bi, i: (bi, i, 0))
    return pl.pallas_call(
        _tail_kernel,
        grid=(b, s // tm),
        in_specs=[
            pl.BlockSpec((1, tm, d), lambda bi, i: (bi, i, 0)),
            br, br, br, br,
            pl.BlockSpec((1, tm, N_BRANCH * d), lambda bi, i: (bi, i, 0)),
            pl.BlockSpec((N_BRANCH, BRANCH_WIDTH, d), lambda bi, i: (0, 0, 0)),
            pl.BlockSpec((d, d), lambda bi, i: (0, 0)),
        ],
        out_specs=pl.BlockSpec((1, tm, d), lambda bi, i: (bi, i, 0)),
        out_shape=jax.ShapeDtypeStruct((b, s, d), F32),
        compiler_params=pltpu.CompilerParams(
            dimension_semantics=("parallel", "parallel"), vmem_limit_bytes=VMEM_LIMIT),
        name="merge_out",
    )(x, ya, yb, yc, ym, mg, wb_bf16, wo_bf16)


def _head_mean_matrix(width, head_dim):
    idx = jnp.arange(width) // head_dim
    return jnp.where(idx[:, None] == idx[None, :], 1.0 / head_dim, 0.0).astype(BF16)


def _block_diag(w):
    n, c, d = w.shape
    eye = jnp.eye(n, dtype=w.dtype)
    return jnp.einsum('ncd,nm->ncmd', w, eye).reshape(n * c, n * d)


def _layer(x, mem, norm_gain, w_in, swa_q_gain, swa_k_gain, swa_sinks, conv_w, conv_b,
           lru_w_a, lru_b_a, lru_w_x, lru_b_x, lru_lambda, mem_norm_gain, w_mem_kv,
           mem_q_gain, mem_k_gain, w_branch, w_out, consts):
    bdq, bdk, tri = consts
    row = lambda v: v.reshape(1, -1).astype(F32)
    (qa, kva, ga, qb, kb, vb, gb, xc, gc, qm, gm, mg) = _inproj(
        x, row(norm_gain), w_in.astype(BF16))

    qg = row(jnp.tile(swa_q_gain, SWA_HEADS) * (SWA_HEAD_DIM ** -0.5))
    kg = row(jnp.tile(swa_k_gain, SWA_KV_HEADS))
    ya = _swa(qa, ga, kva, swa_sinks.reshape(1, SWA_HEADS).astype(F32), qg, kg, bdq, bdk)

    yb = _sb(qb, kb, vb, gb, tri)

    yc = _lru(xc, gc, conv_w.astype(F32), row(conv_b), _block_diag(lru_w_a).astype(BF16),
              row(lru_b_a), _block_diag(lru_w_x).astype(BF16), row(lru_b_x), row(lru_lambda))

    mkt, mv = _memkv(mem, row(mem_norm_gain), w_mem_kv.astype(BF16), row(mem_k_gain))
    ym = _memattn(qm, gm, mkt, mv, row(mem_q_gain))

    return _tail(x, ya, yb, yc, ym, mg, w_branch.astype(BF16), w_out.astype(BF16))


def kernel(x, mem, norm_gain, w_in, swa_q_gain, swa_k_gain, swa_sinks, conv_w, conv_b, lru_w_a,
           lru_b_a, lru_w_x, lru_b_x, lru_lambda, mem_norm_gain, w_mem_kv, mem_q_gain,
           mem_k_gain, w_branch, w_out):
    depth = w_in.shape[0]
    consts = (
        _head_mean_matrix(SWA_HEADS * SWA_HEAD_DIM, SWA_HEAD_DIM),
        _head_mean_matrix(SWA_KV_HEADS * SWA_HEAD_DIM, SWA_HEAD_DIM),
        jnp.tril(jnp.ones((TK_SB, TK_SB), F32), -1).astype(BF16),
    )
    for l in range(depth):
        x = _layer(x, mem, norm_gain[l], w_in[l], swa_q_gain[l], swa_k_gain[l], swa_sinks[l],
                   conv_w[l], conv_b[l], lru_w_a[l], lru_b_a[l], lru_w_x[l], lru_b_x[l],
                   lru_lambda[l], mem_norm_gain[l], w_mem_kv[l], mem_q_gain[l], mem_k_gain[l],
                   w_branch[l], w_out[l], consts)
    return x
```

```python
import jax
import jax.numpy as jnp
from jax import lax
from jax.experimental import pallas as pl
from jax.experimental.pallas import tpu as pltpu

F32 = jnp.float32
BF16 = jnp.bfloat16

D_MODEL = 1024
BRANCH_WIDTH = 512
N_BRANCH = 4
SWA_HEAD_DIM = 64
SWA_HEADS = 8
SWA_KV_HEADS = 2
SWA_WINDOW = 128
SB_HEAD_DIM = 64
LRU_BLOCKS = 8
CONV_WIDTH = 4
LRU_C = 8.0
MEM_HEADS = 4
MEM_HEAD_DIM = 128
EPS = 1e-6

LANES = 128
SUBLANES = 8
HEAD_PAIRS = BRANCH_WIDTH // LANES

_SIZES = (512, 128, 128, 512, 512, 512, 512, 512, 512, 512, 512, 512, 4096)
_OFFS = tuple(sum(_SIZES[:i]) for i in range(len(_SIZES)))
(OFF_AQ, OFF_AK, OFF_AV, OFF_AG, OFF_BQ, OFF_BK, OFF_BV, OFF_BG,
 OFF_CX, OFF_CG, OFF_MQ, OFF_MG, OFF_MERGE) = _OFFS
IN_WIDTH = sum(_SIZES)

TM_PROJ = 512
TM_TAIL = 512
TQ_SWA = 512
TK_SB = 256
TQ_SB = 512
TC_LRU = 512
TM_MEM = 512

VMEM_LIMIT = 56 * 1024 * 1024

SB_EXIT_CARRY = 88.0


def _dot(a, b):
    return jnp.dot(a, b, preferred_element_type=F32)


def _dot_nt(a, b):
    return lax.dot_general(a, b, (((1,), (1,)), ((), ())), preferred_element_type=F32)


def _sigmoid(x):
    return 1.0 / (1.0 + jnp.exp(-x))


def _silu(x):
    return x * _sigmoid(x)


def _inproj_kernel(x_ref, g_ref, w_ref, qa_ref, kva_ref, ga_ref, qb_ref, kb_ref, vb_ref,
                   gb_ref, xc_ref, gc_ref, qm_ref, gm_ref, mg_ref):
    x = x_ref[0]
    ms = jnp.mean(x * x, axis=-1, keepdims=True)
    u = (x * lax.rsqrt(ms + EPS) * g_ref[...]).astype(BF16)

    def proj(off, width):
        return _dot(u, w_ref[:, off:off + width])

    qa_ref[0] = proj(OFF_AQ, 512).astype(BF16)
    kva_ref[0] = proj(OFF_AK, 256).astype(BF16)
    ga_ref[0] = proj(OFF_AG, 512).astype(BF16)
    qb = (proj(OFF_BQ, 512) * (SB_HEAD_DIM ** -0.5)).astype(BF16)
    kb = proj(OFF_BK, 512).astype(BF16)
    vb = proj(OFF_BV, 512).astype(BF16)
    for p in range(HEAD_PAIRS):
        sl = slice(p * LANES, (p + 1) * LANES)
        qb_ref[0, p] = qb[:, sl]
        kb_ref[0, p] = kb[:, sl]
        vb_ref[0, p] = vb[:, sl]
    gb_ref[0] = proj(OFF_BG, 512).astype(BF16)
    xc_ref[0] = proj(OFF_CX, 512).astype(BF16)
    gc_ref[0] = proj(OFF_CG, 512).astype(BF16)
    qm_ref[0] = proj(OFF_MQ, 512).astype(BF16)
    gm_ref[0] = proj(OFF_MG, 512).astype(BF16)
    for n in range(N_BRANCH):
        mg_ref[0, :, n * D_MODEL:(n + 1) * D_MODEL] = proj(
            OFF_MERGE + n * D_MODEL, D_MODEL).astype(BF16)


def _inproj(x, gain, w_bf16):
    b, s, d = x.shape
    tm = TM_PROJ
    row = lambda width: pl.BlockSpec((1, tm, width), lambda bi, i: (bi, i, 0))
    pair = pl.BlockSpec((1, HEAD_PAIRS, tm, LANES), lambda bi, i: (bi, 0, i, 0))
    flat = lambda width: jax.ShapeDtypeStruct((b, s, width), BF16)
    paired = jax.ShapeDtypeStruct((b, HEAD_PAIRS, s, LANES), BF16)
    return pl.pallas_call(
        _inproj_kernel,
        grid=(b, s // tm),
        in_specs=[
            pl.BlockSpec((1, tm, d), lambda bi, i: (bi, i, 0)),
            pl.BlockSpec((1, d), lambda bi, i: (0, 0)),
            pl.BlockSpec((d, IN_WIDTH), lambda bi, i: (0, 0), pipeline_mode=pl.Buffered(1)),
        ],
        out_specs=[row(512), row(256), row(512), pair, pair, pair, row(512),
                   row(512), row(512), row(512), row(512), row(4096)],
        out_shape=[flat(512), flat(256), flat(512), paired, paired, paired, flat(512),
                   flat(512), flat(512), flat(512), flat(512), flat(4096)],
        compiler_params=pltpu.CompilerParams(
            dimension_semantics=("parallel", "parallel"), vmem_limit_bytes=VMEM_LIMIT),
        name="inproj",
    )(x, gain, w_bf16)


def _swa_kernel(sink_ref, q_ref, g_ref, kvc_ref, kvp_ref, qg_ref, kg_ref, bdq_ref, bdk_ref,
                o_ref):
    i = pl.program_id(1)
    tq, sub = TQ_SWA, SWA_WINDOW
    nkeys = tq + sub
    q = q_ref[0].astype(F32)
    msq = _dot((q * q).astype(BF16), bdq_ref[...])
    qn = (q * lax.rsqrt(msq + EPS) * qg_ref[...]).astype(BF16)

    kvp = kvp_ref[0]
    kvc = kvc_ref[0]
    kcat = jnp.concatenate([kvp[:, :LANES], kvc[:, :LANES]], axis=0).astype(F32)
    vcat = jnp.concatenate([kvp[:, LANES:], kvc[:, LANES:]], axis=0).astype(F32)
    msk = _dot((kcat * kcat).astype(BF16), bdk_ref[...])
    kn_t = (kcat * lax.rsqrt(msk + EPS) * kg_ref[...]).T
    zeros = jnp.zeros((SWA_HEAD_DIM, nkeys), F32)

    def k_variant(half, kv):
        blk = kn_t[kv * SWA_HEAD_DIM:(kv + 1) * SWA_HEAD_DIM]
        parts = [blk, zeros] if half == 0 else [zeros, blk]
        return jnp.concatenate(parts, axis=0).astype(BF16)

    kt = [[k_variant(half, kv) for kv in range(SWA_KV_HEADS)] for half in range(2)]
    v_same = vcat.astype(BF16)
    v_swap = pltpu.roll(vcat, SWA_HEAD_DIM, axis=1).astype(BF16)

    r = lax.broadcasted_iota(jnp.int32, (sub, 2 * sub), 0)
    j = lax.broadcasted_iota(jnp.int32, (sub, 2 * sub), 1)
    diff = r - (j - sub)
    band = (diff >= 0) & (diff < SWA_WINDOW)
    first = jnp.where(i > 0, 0, sub)
    band_first = band & (j >= first)
    lane = lax.broadcasted_iota(jnp.int32, (sub, LANES), 1)

    for sb in range(tq // sub):
        rows = slice(sb * sub, (sb + 1) * sub)
        keys = slice(sb * sub, (sb + 2) * sub)
        valid = band_first if sb == 0 else band
        outs = []
        for h in range(SWA_HEADS):
            p, half, kv = h // 2, h % 2, h // (SWA_HEADS // SWA_KV_HEADS)
            sc = _dot(qn[rows, p * LANES:(p + 1) * LANES], kt[half][kv][:, keys])
            sc = jnp.where(valid, sc, -jnp.inf)
            sink = sink_ref[0, h]
            m = jnp.maximum(jnp.max(sc, axis=-1, keepdims=True), sink)
            pe = jnp.exp(sc - m)
            den = jnp.sum(pe, axis=-1, keepdims=True) + jnp.exp(sink - m)
            vv = v_same if half == kv else v_swap
            outs.append(_dot(pe.astype(BF16), vv[keys]) * (1.0 / den))
        blocks = [jnp.where(lane < SWA_HEAD_DIM, outs[2 * p], outs[2 * p + 1])
                  for p in range(HEAD_PAIRS)]
        y = jnp.concatenate(blocks, axis=1)
        o_ref[0, rows, :] = (y * _silu(g_ref[0, rows, :].astype(F32))).astype(BF16)


def _swa(qa, ga, kva, sinks, qg, kg, bdq, bdk):
    b, s, _ = qa.shape
    tq, sub = TQ_SWA, SWA_WINDOW
    return pl.pallas_call(
        _swa_kernel,
        grid=(b, s // tq),
        in_specs=[
            pl.BlockSpec(memory_space=pltpu.SMEM),
            pl.BlockSpec((1, tq, 512), lambda bi, i: (bi, i, 0)),
            pl.BlockSpec((1, tq, 512), lambda bi, i: (bi, i, 0)),
            pl.BlockSpec((1, tq, 256), lambda bi, i: (bi, i, 0)),
            pl.BlockSpec((1, sub, 256),
                         lambda bi, i: (bi, jnp.maximum(i * (tq // sub) - 1, 0), 0)),
            pl.BlockSpec((1, 512), lambda bi, i: (0, 0)),
            pl.BlockSpec((1, LANES), lambda bi, i: (0, 0)),
            pl.BlockSpec((512, 512), lambda bi, i: (0, 0)),
            pl.BlockSpec((LANES, LANES), lambda bi, i: (0, 0)),
        ],
        out_specs=pl.BlockSpec((1, tq, 512), lambda bi, i: (bi, i, 0)),
        out_shape=jax.ShapeDtypeStruct((b, s, 512), BF16),
        compiler_params=pltpu.CompilerParams(
            dimension_semantics=("parallel", "parallel"), vmem_limit_bytes=VMEM_LIMIT),
        name="swa",
    )(sinks, qa, ga, kva, kva, qg, kg, bdq, bdk)


def _sb_tile(q_half, kj, vj, tri, carries, causal):
    tk = kj.shape[0]
    outs, new_carries = [], []
    for a in range(2):
        z = _dot_nt(q_half[a], kj)
        sp = jnp.maximum(z, 0.0) + jnp.log(1.0 + jnp.exp(-jnp.abs(z)))
        lsz = z - sp
        if causal is not None:
            sp = jnp.where(causal, sp, 0.0)
        cum = _dot(sp.astype(BF16), tri)
        carry = carries[a]
        arg = lsz - (cum + jnp.concatenate([carry] * (tk // LANES), axis=1))
        if causal is not None:
            arg = jnp.where(causal, arg, -jnp.inf)
        outs.append(_dot(jnp.exp(arg).astype(BF16), vj))
        new_carries.append(carry + jnp.sum(sp, axis=-1, keepdims=True))
    return outs, new_carries


def _sb_kernel(q_ref, k_ref, v_ref, g_ref, tri_ref, o_ref, acc_ref, car_ref):
    i = pl.program_id(2)
    tb = TK_SB
    nsub = TQ_SB // tb
    lane = lax.broadcasted_iota(jnp.int32, (tb, LANES), 1)
    tri = tri_ref[...]
    r = lax.broadcasted_iota(jnp.int32, (tb, tb), 0)
    c = lax.broadcasted_iota(jnp.int32, (tb, tb), 1)
    causal = c < r

    def q_halves(q):
        zero = jnp.zeros_like(q)
        return [jnp.where(lane < SB_HEAD_DIM, q, zero), jnp.where(lane >= SB_HEAD_DIM, q, zero)]

    def kv_tile(jt):
        start = pl.multiple_of(jt * tb, tb)
        return k_ref[0, 0, pl.ds(start, tb), :], v_ref[0, 0, pl.ds(start, tb), :]

    first = i * nsub
    k0, v0 = kv_tile(jnp.maximum(first - 1, 0))
    tiles = [(k0, jnp.where(first > 0, v0, jnp.zeros_like(v0)))]
    tiles += [kv_tile(first + sb) for sb in range(nsub)]
    zeros = [jnp.zeros((tb, LANES), F32)] * 2
    for sb in range(nsub):
        qh = q_halves(q_ref[0, 0, sb * tb:(sb + 1) * tb, :])
        out_d, car_d = _sb_tile(qh, *tiles[sb + 1], tri, zeros, causal)
        out_l, car_l = _sb_tile(qh, *tiles[sb], tri, car_d, None)
        for a in range(2):
            acc_ref[sb, a] = out_d[a] + out_l[a]
            car_ref[sb, a] = car_l[a]

    def continue_left(sb, unused):
        qh = q_halves(q_ref[0, 0, pl.ds(pl.multiple_of(sb * tb, tb), tb), :])

        def not_done():
            return (jnp.min(car_ref[sb]) < SB_EXIT_CARRY).astype(jnp.int32)

        def cond(state):
            jt, live = state
            return jnp.logical_and(jt >= 0, live > 0)

        def body(state):
            jt, _ = state
            kj, vj = kv_tile(jt)
            outs, cars = _sb_tile(qh, kj, vj, tri, [car_ref[sb, 0], car_ref[sb, 1]], None)
            for a in range(2):
                acc_ref[sb, a] += outs[a]
                car_ref[sb, a] = cars[a]
            return jt - 1, not_done()

        lax.while_loop(cond, body, (first + sb - 2, not_done()))
        return unused

    lax.fori_loop(0, nsub, continue_left, 0)
    for sb in range(nsub):
        rows = slice(sb * tb, (sb + 1) * tb)
        y = jnp.where(lane < SB_HEAD_DIM, acc_ref[sb, 0], acc_ref[sb, 1])
        o_ref[0, rows, :] = (y * _silu(g_ref[0, rows, :].astype(F32))).astype(BF16)


def _sb(qb, kb, vb, gb, tri):
    b, _, s, _ = qb.shape
    tq, tb = TQ_SB, TK_SB
    return pl.pallas_call(
        _sb_kernel,
        grid=(b, HEAD_PAIRS, s // tq),
        in_specs=[
            pl.BlockSpec((1, 1, tq, LANES), lambda bi, p, i: (bi, p, i, 0)),
            pl.BlockSpec((1, 1, s, LANES), lambda bi, p, i: (bi, p, 0, 0)),
            pl.BlockSpec((1, 1, s, LANES), lambda bi, p, i: (bi, p, 0, 0)),
            pl.BlockSpec((1, tq, LANES), lambda bi, p, i: (bi, i, p)),
            pl.BlockSpec((tb, tb), lambda bi, p, i: (0, 0)),
        ],
        out_specs=pl.BlockSpec((1, tq, LANES), lambda bi, p, i: (bi, i, p)),
        out_shape=jax.ShapeDtypeStruct((b, s, BRANCH_WIDTH), BF16),
        scratch_shapes=[pltpu.VMEM((tq // tb, 2, tb, LANES), F32),
                        pltpu.VMEM((tq // tb, 2, tb, LANES), F32)],
        compiler_params=pltpu.CompilerParams(
            dimension_semantics=("parallel", "parallel", "parallel"),
            vmem_limit_bytes=VMEM_LIMIT),
        name="stickbreak",
    )(qb, kb, vb, gb, tri)


def _lru_kernel(x_ref, g_ref, cw_ref, cb_ref, wa_ref, ba_ref, wx_ref, bx_ref, lam_ref, o_ref,
                xe_ref, h_ref, hs_ref):
    i = pl.program_id(1)
    tc = TC_LRU
    halo = SUBLANES

    @pl.when(i == 0)
    def _():
        xe_ref[0:halo, :] = jnp.zeros((halo, BRANCH_WIDTH), F32)
        h_ref[...] = jnp.zeros_like(h_ref)

    x = x_ref[0].astype(F32)
    xe_ref[halo:halo + tc, :] = x
    xc = cb_ref[...] + cw_ref[CONV_WIDTH - 1:CONV_WIDTH, :] * x
    for tap in range(CONV_WIDTH - 1):
        back = CONV_WIDTH - 1 - tap
        xc = xc + cw_ref[tap:tap + 1, :] * xe_ref[halo - back:halo - back + tc, :]
    xe_ref[0:halo, :] = x[tc - halo:, :]

    xb = xc.astype(BF16)
    rg = _sigmoid(_dot(xb, wa_ref[...]) + ba_ref[...])
    ig = _sigmoid(_dot(xb, wx_ref[...]) + bx_ref[...])
    lam = lam_ref[...]
    softplus_neg_lam = jnp.maximum(-lam, 0.0) + jnp.log1p(jnp.exp(-jnp.abs(lam)))
    a = jnp.exp(-LRU_C * rg * softplus_neg_lam)
    bv = jnp.sqrt(1.0 - a * a) * (ig * xc)

    groups = tc // SUBLANES
    a3 = a.reshape(groups, SUBLANES, BRANCH_WIDTH)
    b3 = bv.reshape(groups, SUBLANES, BRANCH_WIDTH)
    sub = lax.broadcasted_iota(jnp.int32, (groups, SUBLANES, BRANCH_WIDTH), 1)
    shift = 1
    while shift < SUBLANES:
        keep = sub >= shift
        a_sh = pltpu.roll(a3, shift, axis=1)
        b_sh = pltpu.roll(b3, shift, axis=1)
        b3 = jnp.where(keep, a3 * b_sh + b3, b3)
        a3 = jnp.where(keep, a3 * a_sh, a3)
        shift *= 2
    h = h_ref[...]
    for gi in range(groups):
        hg = a3[gi] * h + b3[gi]
        hs_ref[gi * SUBLANES:(gi + 1) * SUBLANES, :] = hg
        h = hg[SUBLANES - 1:SUBLANES, :]
    h_ref[...] = h
    o_ref[0] = (hs_ref[...] * _silu(g_ref[0].astype(F32))).astype(BF16)


def _lru(xc, gc, cw, cb, wa, ba, wx, bx, lam):
    b, s, w = xc.shape
    tc = TC_LRU
    vec = pl.BlockSpec((1, w), lambda bi, i: (0, 0))
    mat = pl.BlockSpec((w, w), lambda bi, i: (0, 0))
    return pl.pallas_call(
        _lru_kernel,
        grid=(b, s // tc),
        in_specs=[
            pl.BlockSpec((1, tc, w), lambda bi, i: (bi, i, 0)),
            pl.BlockSpec((1, tc, w), lambda bi, i: (bi, i, 0)),
            pl.BlockSpec((CONV_WIDTH, w), lambda bi, i: (0, 0)),
            vec, mat, vec, mat, vec, vec,
        ],
        out_specs=pl.BlockSpec((1, tc, w), lambda bi, i: (bi, i, 0)),
        out_shape=jax.ShapeDtypeStruct((b, s, w), BF16),
        scratch_shapes=[pltpu.VMEM((tc + SUBLANES, w), F32), pltpu.VMEM((1, w), F32),
                        pltpu.VMEM((tc, w), F32)],
        compiler_params=pltpu.CompilerParams(
            dimension_semantics=("parallel", "arbitrary"), vmem_limit_bytes=VMEM_LIMIT),
        name="rglru",
    )(xc, gc, cw, cb, wa, ba, wx, bx, lam)


def _memkv_kernel(mem_ref, g_ref, w_ref, kg_ref, kt_ref, v_ref):
    m = mem_ref[0]
    ms = jnp.mean(m * m, axis=-1, keepdims=True)
    u = (m * lax.rsqrt(ms + EPS) * g_ref[...]).astype(BF16)
    kv = _dot(u, w_ref[...])
    for h in range(MEM_HEADS):
        kh = kv[:, h * MEM_HEAD_DIM:(h + 1) * MEM_HEAD_DIM]
        msk = jnp.mean(kh * kh, axis=-1, keepdims=True)
        kn = kh * lax.rsqrt(msk + EPS) * kg_ref[...]
        kt_ref[0, h * MEM_HEAD_DIM:(h + 1) * MEM_HEAD_DIM, :] = kn.T.astype(BF16)
    v_ref[0] = kv[:, BRANCH_WIDTH:].astype(BF16)


def _memkv(mem, gain, w_bf16, kg):
    b, mlen, d = mem.shape
    return pl.pallas_call(
        _memkv_kernel,
        grid=(b,),
        in_specs=[
            pl.BlockSpec((1, mlen, d), lambda bi: (bi, 0, 0)),
            pl.BlockSpec((1, d), lambda bi: (0, 0)),
            pl.BlockSpec((d, 2 * BRANCH_WIDTH), lambda bi: (0, 0)),
            pl.BlockSpec((1, MEM_HEAD_DIM), lambda bi: (0, 0)),
        ],
        out_specs=[pl.BlockSpec((1, BRANCH_WIDTH, mlen), lambda bi: (bi, 0, 0)),
                   pl.BlockSpec((1, mlen, BRANCH_WIDTH), lambda bi: (bi, 0, 0))],
        out_shape=[jax.ShapeDtypeStruct((b, BRANCH_WIDTH, mlen), BF16),
                   jax.ShapeDtypeStruct((b, mlen, BRANCH_WIDTH), BF16)],
        compiler_params=pltpu.CompilerParams(
            dimension_semantics=("parallel",), vmem_limit_bytes=VMEM_LIMIT),
        name="memkv",
    )(mem, gain, w_bf16, kg)


def _memattn_kernel(q_ref, g_ref, kt_ref, v_ref, qg_ref, o_ref):
    scale = MEM_HEAD_DIM ** -0.5
    outs = []
    for h in range(MEM_HEADS):
        sl = slice(h * MEM_HEAD_DIM, (h + 1) * MEM_HEAD_DIM)
        qh = q_ref[0, :, sl].astype(F32)
        ms = jnp.mean(qh * qh, axis=-1, keepdims=True)
        qn = (qh * lax.rsqrt(ms + EPS) * qg_ref[...]).astype(BF16)
        sc = _dot(qn, kt_ref[0, sl, :]) * scale
        m = jnp.max(sc, axis=-1, keepdims=True)
        pe = jnp.exp(sc - m)
        den = jnp.sum(pe, axis=-1, keepdims=True)
        outs.append(_dot(pe.astype(BF16), v_ref[0, :, sl]) * (1.0 / den))
    y = jnp.concatenate(outs, axis=1)
    o_ref[0] = (y * _silu(g_ref[0].astype(F32))).astype(BF16)


def _memattn(qm, gm, mkt, mv, qg):
    b, s, w = qm.shape
    mlen = mv.shape[1]
    tm = TM_MEM
    return pl.pallas_call(
        _memattn_kernel,
        grid=(b, s // tm),
        in_specs=[
            pl.BlockSpec((1, tm, w), lambda bi, i: (bi, i, 0)),
            pl.BlockSpec((1, tm, w), lambda bi, i: (bi, i, 0)),
            pl.BlockSpec((1, w, mlen), lambda bi, i: (bi, 0, 0)),
            pl.BlockSpec((1, mlen, w), lambda bi, i: (bi, 0, 0)),
            pl.BlockSpec((1, MEM_HEAD_DIM), lambda bi, i: (0, 0)),
        ],
        out_specs=pl.BlockSpec((1, tm, w), lambda bi, i: (bi, i, 0)),
        out_shape=jax.ShapeDtypeStruct((b, s, w), BF16),
        compiler_params=pltpu.CompilerParams(
            dimension_semantics=("parallel", "parallel"), vmem_limit_bytes=VMEM_LIMIT),
        name="memattn",
    )(qm, gm, mkt, mv, qg)


def _tail_kernel(x_ref, ya_ref, yb_ref, yc_ref, ym_ref, mg_ref, wb_ref, wo_ref, o_ref):
    mixed = None
    for n, y_ref in enumerate((ya_ref, yb_ref, yc_ref, ym_ref)):
        up = _dot(y_ref[0], wb_ref[n])
        gate = _sigmoid(mg_ref[0, :, n * D_MODEL:(n + 1) * D_MODEL].astype(F32))
        mixed = gate * up if mixed is None else mixed + gate * up
    o_ref[0] = x_ref[0] + _dot(mixed.astype(BF16), wo_ref[...])


def _tail(x, ya, yb, yc, ym, mg, wb_bf16, wo_bf16):
    b, s, d = x.shape
    tm = TM_TAIL
    br = pl.BlockSpec((1, tm, BRANCH_WIDTH), lambda bi, i: (bi, i, 0))
    return pl.pallas_call(
        _tail_kernel,
        grid=(b, s // tm),
        in_specs=[
            pl.BlockSpec((1, tm, d), lambda bi, i: (bi, i, 0)),
            br, br, br, br,
            pl.BlockSpec((1, tm, N_BRANCH * d), lambda bi, i: (bi, i, 0)),
            pl.BlockSpec((N_BRANCH, BRANCH_WIDTH, d), lambda bi, i: (0, 0, 0)),
            pl.BlockSpec((d, d), lambda bi, i: (0, 0)),
        ],
        out_specs=pl.BlockSpec((1, tm, d), lambda bi, i: (bi, i, 0)),
        out_shape=jax.ShapeDtypeStruct((b, s, d), F32),
        compiler_params=pltpu.CompilerParams(
            dimension_semantics=("parallel", "parallel"), vmem_limit_bytes=VMEM_LIMIT),
        name="merge_out",
    )(x, ya, yb, yc, ym, mg, wb_bf16, wo_bf16)


def _head_mean_matrix(width, head_dim):
    idx = jnp.arange(width) // head_dim
    return jnp.where(idx[:, None] == idx[None, :], 1.0 / head_dim, 0.0).astype(BF16)


def _block_diag(w):
    n, c, d = w.shape
    eye = jnp.eye(n, dtype=w.dtype)
    return jnp.einsum('ncd,nm->ncmd', w, eye).reshape(n * c, n * d)


def _layer(x, mem, norm_gain, w_in, swa_q_gain, swa_k_gain, swa_sinks, conv_w, conv_b,
           lru_w_a, lru_b_a, lru_w_x, lru_b_x, lru_lambda, mem_norm_gain, w_mem_kv,
           mem_q_gain, mem_k_gain, w_branch, w_out, consts):
    bdq, bdk, tri = consts
    row = lambda v: v.reshape(1, -1).astype(F32)
    (qa, kva, ga, qb, kb, vb, gb, xc, gc, qm, gm, mg) = _inproj(
        x, row(norm_gain), w_in.astype(BF16))

    qg = row(jnp.tile(swa_q_gain, SWA_HEADS) * (SWA_HEAD_DIM ** -0.5))
    kg = row(jnp.tile(swa_k_gain, SWA_KV_HEADS))
    ya = _swa(qa, ga, kva, swa_sinks.reshape(1, SWA_HEADS).astype(F32), qg, kg, bdq, bdk)

    yb = _sb(qb, kb, vb, gb, tri)

    yc = _lru(xc, gc, conv_w.astype(F32), row(conv_b), _block_diag(lru_w_a).astype(BF16),
              row(lru_b_a), _block_diag(lru_w_x).astype(BF16), row(lru_b_x), row(lru_lambda))

    mkt, mv = _memkv(mem, row(mem_norm_gain), w_mem_kv.astype(BF16), row(mem_k_gain))
    ym = _memattn(qm, gm, mkt, mv, row(mem_q_gain))

    return _tail(x, ya, yb, yc, ym, mg, w_branch.astype(BF16), w_out.astype(BF16))


def kernel(x, mem, norm_gain, w_in, swa_q_gain, swa_k_gain, swa_sinks, conv_w, conv_b, lru_w_a,
           lru_b_a, lru_w_x, lru_b_x, lru_lambda, mem_norm_gain, w_mem_kv, mem_q_gain,
           mem_k_gain, w_branch, w_out):
    depth = w_in.shape[0]
    consts = (
        _head_mean_matrix(SWA_HEADS * SWA_HEAD_DIM, SWA_HEAD_DIM),
        _head_mean_matrix(SWA_KV_HEADS * SWA_HEAD_DIM, SWA_HEAD_DIM),
        jnp.tril(jnp.ones((TK_SB, TK_SB), F32), -1).astype(BF16),
    )
    for l in range(depth):
        x = _layer(x, mem, norm_gain[l], w_in[l], swa_q_gain[l], swa_k_gain[l], swa_sinks[l],
                   conv_w[l], conv_b[l], lru_w_a[l], lru_b_a[l], lru_w_x[l], lru_b_x[l],
                   lru_lambda[l], mem_norm_gain[l], w_mem_kv[l], mem_q_gain[l], mem_k_gain[l],
                   w_branch[l], w_out[l], consts)
    return x
```

```python
import jax
import jax.numpy as jnp
from jax import lax
from jax.experimental import pallas as pl
from jax.experimental.pallas import tpu as pltpu

F32 = jnp.float32
BF16 = jnp.bfloat16

D_MODEL = 1024
BRANCH_WIDTH = 512
N_BRANCH = 4
SWA_HEAD_DIM = 64
SWA_HEADS = 8
SWA_KV_HEADS = 2
SWA_WINDOW = 128
SB_HEAD_DIM = 64
LRU_BLOCKS = 8
CONV_WIDTH = 4
LRU_C = 8.0
MEM_HEADS = 4
MEM_HEAD_DIM = 128
EPS = 1e-6

LANES = 128
SUBLANES = 8
HEAD_PAIRS = BRANCH_WIDTH // LANES

_SIZES = (512, 128, 128, 512, 512, 512, 512, 512, 512, 512, 512, 512, 4096)
_OFFS = tuple(sum(_SIZES[:i]) for i in range(len(_SIZES)))
(OFF_AQ, OFF_AK, OFF_AV, OFF_AG, OFF_BQ, OFF_BK, OFF_BV, OFF_BG,
 OFF_CX, OFF_CG, OFF_MQ, OFF_MG, OFF_MERGE) = _OFFS
IN_WIDTH = sum(_SIZES)

TM_FRONT = 512
TM_TAIL = 512
TK_SB = 256
TQ_SB = 512
FILL_COLS = 512

VMEM_LIMIT = 58 * 1024 * 1024

SB_EXIT_CARRY = 88.0


def _dot(a, b):
    return jnp.dot(a, b, preferred_element_type=F32)


def _dot_nt(a, b):
    return lax.dot_general(a, b, (((1,), (1,)), ((), ())), preferred_element_type=F32)


def _sigmoid(x):
    return 1.0 / (1.0 + jnp.exp(-x))


def _silu(x):
    return x * _sigmoid(x)


def _swa_branch(sink_ref, q, g, kcat, vcat, qg, kg, bdq, bdk, at_start, o_ref, fill):
    tq, sub = q.shape[0], SWA_WINDOW
    nkeys = tq + sub
    msq = _dot((q * q).astype(BF16), bdq)
    qn = (q * lax.rsqrt(msq + EPS) * qg).astype(BF16)
    msk = _dot((kcat * kcat).astype(BF16), bdk)
    kn_t = (kcat * lax.rsqrt(msk + EPS) * kg).T
    zeros = jnp.zeros((SWA_HEAD_DIM, nkeys), F32)

    def k_variant(half, kv):
        blk = kn_t[kv * SWA_HEAD_DIM:(kv + 1) * SWA_HEAD_DIM]
        parts = [blk, zeros] if half == 0 else [zeros, blk]
        return jnp.concatenate(parts, axis=0).astype(BF16)

    kt = [[k_variant(half, kv) for kv in range(SWA_KV_HEADS)] for half in range(2)]
    v_same = vcat.astype(BF16)
    v_swap = pltpu.roll(vcat, SWA_HEAD_DIM, axis=1).astype(BF16)

    r = lax.broadcasted_iota(jnp.int32, (sub, 2 * sub), 0)
    j = lax.broadcasted_iota(jnp.int32, (sub, 2 * sub), 1)
    diff = r - (j - sub)
    band = (diff >= 0) & (diff < SWA_WINDOW)
    first = jnp.where(at_start, sub, 0)
    band_first = band & (j >= first)
    lane = lax.broadcasted_iota(jnp.int32, (sub, LANES), 1)
    row2 = lax.broadcasted_iota(jnp.int32, (2 * sub, 1), 0)

    for sb in range(tq // sub):
        rows = slice(sb * sub, (sb + 1) * sub)
        keys = slice(sb * sub, (sb + 2) * sub)
        valid = band_first if sb == 0 else band
        valid2 = jnp.concatenate([valid, valid], axis=0)
        outs = [None] * SWA_HEADS
        for kv in range(SWA_KV_HEADS):
            for half in range(2):
                pa, pb = 2 * kv, 2 * kv + 1
                ha, hb = 2 * pa + half, 2 * pb + half
                q2 = jnp.concatenate([qn[rows, pa * LANES:(pa + 1) * LANES],
                                      qn[rows, pb * LANES:(pb + 1) * LANES]], axis=0)
                sc = _dot(q2, kt[half][kv][:, keys])
                sc = jnp.where(valid2, sc, -jnp.inf)
                sink = jnp.where(row2 < sub, sink_ref[0, ha], sink_ref[0, hb])
                m = jnp.maximum(jnp.max(sc, axis=-1, keepdims=True), sink)
                pe = jnp.exp(sc - m)
                den = jnp.sum(pe, axis=-1, keepdims=True) + jnp.exp(sink - m)
                vv = v_same if half == kv else v_swap
                res = _dot(pe.astype(BF16), vv[keys]) * (1.0 / den)
                outs[ha], outs[hb] = res[:sub], res[sub:]
        blocks = [jnp.where(lane < SWA_HEAD_DIM, outs[2 * p], outs[2 * p + 1])
                  for p in range(HEAD_PAIRS)]
        y = jnp.concatenate(blocks, axis=1)
        o_ref[0, rows, :] = (y * _silu(g[rows])).astype(BF16)
        fill()


def _lru_branch(x, g, cw_ref, cb_ref, wa_ref, ba_ref, wx_ref, bx_ref, lam_ref,
                xe_ref, h_ref, hs_ref, fill):
    tc = x.shape[0]
    halo = SUBLANES
    xe_ref[halo:halo + tc, :] = x
    xc = cb_ref[...] + cw_ref[CONV_WIDTH - 1:CONV_WIDTH, :] * x
    for tap in range(CONV_WIDTH - 1):
        back = CONV_WIDTH - 1 - tap
        xc = xc + cw_ref[tap:tap + 1, :] * xe_ref[halo - back:halo - back + tc, :]
    xe_ref[0:halo, :] = x[tc - halo:, :]

    xb = xc.astype(BF16)
    rg = _sigmoid(_dot(xb, wa_ref[...]) + ba_ref[...])
    ig = _sigmoid(_dot(xb, wx_ref[...]) + bx_ref[...])
    lam = lam_ref[...]
    softplus_neg_lam = jnp.maximum(-lam, 0.0) + jnp.log1p(jnp.exp(-jnp.abs(lam)))
    a = jnp.exp(-LRU_C * rg * softplus_neg_lam)
    bv = jnp.sqrt(1.0 - a * a) * (ig * xc)
    fill()

    groups = tc // SUBLANES
    a3 = a.reshape(groups, SUBLANES, BRANCH_WIDTH)
    b3 = bv.reshape(groups, SUBLANES, BRANCH_WIDTH)
    sub = lax.broadcasted_iota(jnp.int32, (groups, SUBLANES, BRANCH_WIDTH), 1)
    shift = 1
    while shift < SUBLANES:
        keep = sub >= shift
        a_sh = pltpu.roll(a3, shift, axis=1)
        b_sh = pltpu.roll(b3, shift, axis=1)
        b3 = jnp.where(keep, a3 * b_sh + b3, b3)
        a3 = jnp.where(keep, a3 * a_sh, a3)
        shift *= 2
    fill()
    h = h_ref[...]
    for gi in range(groups):
        hg = a3[gi] * h + b3[gi]
        hs_ref[gi * SUBLANES:(gi + 1) * SUBLANES, :] = hg
        h = hg[SUBLANES - 1:SUBLANES, :]
        if gi % (groups // 2) == groups // 2 - 1:
            fill()
    h_ref[...] = h
    return hs_ref[...] * _silu(g)


def _mem_branch(q, g, kt_ref, v_ref, qg, fill):
    scale = MEM_HEAD_DIM ** -0.5
    outs = []
    for h in range(MEM_HEADS):
        sl = slice(h * MEM_HEAD_DIM, (h + 1) * MEM_HEAD_DIM)
        qh = q[:, sl]
        ms = jnp.mean(qh * qh, axis=-1, keepdims=True)
        qn = (qh * lax.rsqrt(ms + EPS) * qg).astype(BF16)
        sc = _dot(qn, kt_ref[0, sl, :]) * scale
        m = jnp.max(sc, axis=-1, keepdims=True)
        pe = jnp.exp(sc - m)
        den = jnp.sum(pe, axis=-1, keepdims=True)
        outs.append(_dot(pe.astype(BF16), v_ref[0, :, sl]) * (1.0 / den))
        fill()
    return jnp.concatenate(outs, axis=1) * _silu(g)


def _memkv_kernel(mem_ref, g_ref, w_ref, kg_ref, kt_ref, v_ref):
    m = mem_ref[0]
    ms = jnp.mean(m * m, axis=-1, keepdims=True)
    u = (m * lax.rsqrt(ms + EPS) * g_ref[...]).astype(BF16)
    kv = _dot(u, w_ref[...])
    for h in range(MEM_HEADS):
        kh = kv[:, h * MEM_HEAD_DIM:(h + 1) * MEM_HEAD_DIM]
        msk = jnp.mean(kh * kh, axis=-1, keepdims=True)
        kn = kh * lax.rsqrt(msk + EPS) * kg_ref[...]
        kt_ref[0, h * MEM_HEAD_DIM:(h + 1) * MEM_HEAD_DIM, :] = kn.T.astype(BF16)
    v_ref[0] = kv[:, BRANCH_WIDTH:].astype(BF16)


def _memkv(mem, gain, w_bf16, kg):
    b, mlen, d = mem.shape
    return pl.pallas_call(
        _memkv_kernel,
        grid=(b,),
        in_specs=[
            pl.BlockSpec((1, mlen, d), lambda bi: (bi, 0, 0)),
            pl.BlockSpec((1, d), lambda bi: (0, 0)),
            pl.BlockSpec((d, 2 * BRANCH_WIDTH), lambda bi: (0, 0)),
            pl.BlockSpec((1, MEM_HEAD_DIM), lambda bi: (0, 0)),
        ],
        out_specs=[pl.BlockSpec((1, BRANCH_WIDTH, mlen), lambda bi: (bi, 0, 0)),
                   pl.BlockSpec((1, mlen, BRANCH_WIDTH), lambda bi: (bi, 0, 0))],
        out_shape=[jax.ShapeDtypeStruct((b, BRANCH_WIDTH, mlen), BF16),
                   jax.ShapeDtypeStruct((b, mlen, BRANCH_WIDTH), BF16)],
        compiler_params=pltpu.CompilerParams(
            dimension_semantics=("parallel",), vmem_limit_bytes=VMEM_LIMIT),
        name="memkv",
    )(mem, gain, w_bf16, kg)


def _front_kernel(sink_ref, x_ref, g_ref, w_ref, qg_ref, kg_ref, bdq_ref, bdk_ref,
                  cw_ref, cb_ref, wa_ref, ba_ref, wx_ref, bx_ref, lam_ref,
                  mkt_ref, mv_ref, mqg_ref,
                  ya_ref, yc_ref, ym_ref, qb_ref, kb_ref, vb_ref, gb_ref, mg_ref,
                  kvp_ref, xe_ref, h_ref, hs_ref):
    i = pl.program_id(1)
    tm = TM_FRONT

    @pl.when(i == 0)
    def _():
        kvp_ref[...] = jnp.zeros_like(kvp_ref)
        xe_ref[0:SUBLANES, :] = jnp.zeros((SUBLANES, BRANCH_WIDTH), F32)
        h_ref[...] = jnp.zeros_like(h_ref)

    x = x_ref[0]
    ms = jnp.mean(x * x, axis=-1, keepdims=True)
    u = (x * lax.rsqrt(ms + EPS) * g_ref[...]).astype(BF16)

    def proj(off, width):
        return _dot(u, w_ref[:, off:off + width])

    def store_pairs(ref, off, scale):
        def emit():
            val = proj(off, BRANCH_WIDTH)
            if scale is not None:
                val = val * scale
            val = val.astype(BF16)
            for p in range(HEAD_PAIRS):
                ref[0, p] = val[:, p * LANES:(p + 1) * LANES]
        return emit

    def store_cols(ref, off, col, width):
        def emit():
            ref[0, :, col:col + width] = proj(off, width).astype(BF16)
        return emit

    pending = [store_pairs(qb_ref, OFF_BQ, SB_HEAD_DIM ** -0.5), store_pairs(kb_ref, OFF_BK, None),
               store_pairs(vb_ref, OFF_BV, None), store_cols(gb_ref, OFF_BG, 0, BRANCH_WIDTH)]
    pending += [store_cols(mg_ref, OFF_MERGE + c, c, FILL_COLS)
                for c in range(0, N_BRANCH * D_MODEL, FILL_COLS)]
    pending = iter(pending)

    def fill():
        emit = next(pending, None)
        if emit is not None:
            emit()

    kva = proj(OFF_AK, 2 * LANES)
    kvp = kvp_ref[...]
    kcat = jnp.concatenate([kvp[:, :LANES], kva[:, :LANES]], axis=0)
    vcat = jnp.concatenate([kvp[:, LANES:], kva[:, LANES:]], axis=0)
    kvp_ref[...] = kva[tm - SWA_WINDOW:, :]
    _swa_branch(sink_ref, proj(OFF_AQ, 512), proj(OFF_AG, 512), kcat, vcat, qg_ref[...],
                kg_ref[...], bdq_ref[...], bdk_ref[...], i == 0, ya_ref, fill)

    yc = _lru_branch(proj(OFF_CX, 512), proj(OFF_CG, 512), cw_ref, cb_ref, wa_ref, ba_ref,
                     wx_ref, bx_ref, lam_ref, xe_ref, h_ref, hs_ref, fill)
    yc_ref[0] = yc.astype(BF16)

    ym = _mem_branch(proj(OFF_MQ, 512), proj(OFF_MG, 512), mkt_ref, mv_ref, mqg_ref[...], fill)
    ym_ref[0] = ym.astype(BF16)
    for emit in pending:
        emit()


def _front(x, sinks, gain, w_bf16, qg, kg, bdq, bdk, cw, cb, wa, ba, wx, bx, lam, mkt, mv, mqg):
    b, s, d = x.shape
    w = BRANCH_WIDTH
    mlen = mv.shape[1]
    tm = TM_FRONT
    const = lambda *shape: pl.BlockSpec(shape, lambda bi, i: (0,) * len(shape))
    row = lambda width: pl.BlockSpec((1, tm, width), lambda bi, i: (bi, i, 0))
    pair = pl.BlockSpec((1, HEAD_PAIRS, tm, LANES), lambda bi, i: (bi, 0, i, 0))
    flat = lambda width: jax.ShapeDtypeStruct((b, s, width), BF16)
    paired = jax.ShapeDtypeStruct((b, HEAD_PAIRS, s, LANES), BF16)
    return pl.pallas_call(
        _front_kernel,
        grid=(b, s // tm),
        in_specs=[
            pl.BlockSpec(memory_space=pltpu.SMEM),
            pl.BlockSpec((1, tm, d), lambda bi, i: (bi, i, 0)),
            const(1, d),
            pl.BlockSpec((d, IN_WIDTH), lambda bi, i: (0, 0), pipeline_mode=pl.Buffered(1)),
            const(1, w), const(1, LANES), const(w, w), const(LANES, LANES),
            const(CONV_WIDTH, w), const(1, w), const(w, w), const(1, w), const(w, w),
            const(1, w), const(1, w),
            pl.BlockSpec((1, w, mlen), lambda bi, i: (bi, 0, 0)),
            pl.BlockSpec((1, mlen, w), lambda bi, i: (bi, 0, 0)),
            const(1, MEM_HEAD_DIM),
        ],
        out_specs=[row(w), row(w), row(w), pair, pair, pair, row(w), row(N_BRANCH * d)],
        out_shape=[flat(w), flat(w), flat(w), paired, paired, paired, flat(w),
                   flat(N_BRANCH * d)],
        scratch_shapes=[pltpu.VMEM((SWA_WINDOW, 2 * LANES), F32),
                        pltpu.VMEM((tm + SUBLANES, w), F32),
                        pltpu.VMEM((1, w), F32),
                        pltpu.VMEM((tm, w), F32)],
        compiler_params=pltpu.CompilerParams(
            dimension_semantics=("parallel", "arbitrary"), vmem_limit_bytes=VMEM_LIMIT),
        name="front",
    )(sinks, x, gain, w_bf16, qg, kg, bdq, bdk, cw, cb, wa, ba, wx, bx, lam, mkt, mv, mqg)


def _sb_tile(q_half, kj, vj, tri_incl, carries, causal, causal_bf16):
    rows, tk = q_half[0].shape[0], kj.shape[0]
    outs, new_carries = [], []
    for a in range(2):
        z = _dot_nt(q_half[a], kj)
        zb = z.astype(BF16)
        sp = jnp.maximum(zb, 0) + jnp.log(1 + jnp.exp(-jnp.abs(zb)))
        if causal_bf16 is not None:
            sp = sp * causal_bf16
        cum = _dot(sp, tri_incl)
        carry = carries[a]
        arg = z - (cum + jnp.concatenate([carry] * (tk // LANES), axis=1))
        if causal is not None:
            arg = jnp.where(causal, arg, -jnp.inf)
        outs.append(_dot(jnp.exp(arg).astype(BF16), vj))
        new_carries.append(carry + jnp.broadcast_to(cum[:, 0:1], (rows, LANES)))
    return outs, new_carries


def _sb_kernel(q_ref, k_ref, v_ref, g_ref, tri_ref, o_ref, acc_ref, car_ref):
    i = pl.program_id(2)
    tb = TK_SB
    nsub = TQ_SB // tb
    lane = lax.broadcasted_iota(jnp.int32, (tb, LANES), 1)
    tri = tri_ref[0]
    strict = tri_ref[1]
    r = lax.broadcasted_iota(jnp.int32, (tb, tb), 0)
    c = lax.broadcasted_iota(jnp.int32, (tb, tb), 1)
    causal = c < r

    def q_halves(q):
        zero = jnp.zeros_like(q)
        return [jnp.where(lane < SB_HEAD_DIM, q, zero), jnp.where(lane >= SB_HEAD_DIM, q, zero)]

    def kv_tile(jt):
        start = pl.multiple_of(jt * tb, tb)
        return k_ref[0, 0, pl.ds(start, tb), :], v_ref[0, 0, pl.ds(start, tb), :]

    first = i * nsub
    k0, v0 = kv_tile(jnp.maximum(first - 1, 0))
    tiles = [(k0, jnp.where(first > 0, v0, jnp.zeros_like(v0)))]
    tiles += [kv_tile(first + sb) for sb in range(nsub)]
    zeros = [jnp.zeros((tb, LANES), F32)] * 2
    for sb in range(nsub):
        qh = q_halves(q_ref[0, 0, sb * tb:(sb + 1) * tb, :])
        out_d, car_d = _sb_tile(qh, *tiles[sb + 1], tri, zeros, causal, strict)
        out_l, car_l = _sb_tile(qh, *tiles[sb], tri, car_d, None, None)
        for a in range(2):
            acc_ref[sb, a] = out_d[a] + out_l[a]
            car_ref[sb, a] = car_l[a]

    def continue_left(sb, unused):
        qh = q_halves(q_ref[0, 0, pl.ds(pl.multiple_of(sb * tb, tb), tb), :])

        def not_done():
            return (jnp.min(car_ref[sb]) < SB_EXIT_CARRY).astype(jnp.int32)

        def cond(state):
            jt, live = state
            return jnp.logical_and(jt >= 0, live > 0)

        def body(state):
            jt, _ = state
            kj, vj = kv_tile(jt)
            outs, cars = _sb_tile(qh, kj, vj, tri, [car_ref[sb, 0], car_ref[sb, 1]], None, None)
            for a in range(2):
                acc_ref[sb, a] += outs[a]
                car_ref[sb, a] = cars[a]
            return jt - 1, not_done()

        lax.while_loop(cond, body, (first + sb - 2, not_done()))
        return unused

    @pl.when(jnp.min(car_ref[...]) < SB_EXIT_CARRY)
    def _():
        lax.fori_loop(0, nsub, continue_left, 0)

    for sb in range(nsub):
        rows = slice(sb * tb, (sb + 1) * tb)
        y = jnp.where(lane < SB_HEAD_DIM, acc_ref[sb, 0], acc_ref[sb, 1])
        o_ref[0, rows, :] = (y * _silu(g_ref[0, rows, :].astype(F32))).astype(BF16)


def _sb(qb, kb, vb, gb, tri):
    b, _, s, _ = qb.shape
    tq, tb = TQ_SB, TK_SB
    return pl.pallas_call(
        _sb_kernel,
        grid=(b, HEAD_PAIRS, s // tq),
        in_specs=[
            pl.BlockSpec((1, 1, tq, LANES), lambda bi, p, i: (bi, p, i, 0)),
            pl.BlockSpec((1, 1, s, LANES), lambda bi, p, i: (bi, p, 0, 0)),
            pl.BlockSpec((1, 1, s, LANES), lambda bi, p, i: (bi, p, 0, 0)),
            pl.BlockSpec((1, tq, LANES), lambda bi, p, i: (bi, i, p)),
            pl.BlockSpec((2, tb, tb), lambda bi, p, i: (0, 0, 0)),
        ],
        out_specs=pl.BlockSpec((1, tq, LANES), lambda bi, p, i: (bi, i, p)),
        out_shape=jax.ShapeDtypeStruct((b, s, BRANCH_WIDTH), BF16),
        scratch_shapes=[pltpu.VMEM((tq // tb, 2, tb, LANES), F32),
                        pltpu.VMEM((tq // tb, 2, tb, LANES), F32)],
        compiler_params=pltpu.CompilerParams(
            dimension_semantics=("parallel", "parallel", "parallel"),
            vmem_limit_bytes=VMEM_LIMIT),
        name="stickbreak",
    )(qb, kb, vb, gb, tri)


def _tail_kernel(x_ref, ya_ref, yb_ref, yc_ref, ym_ref, mg_ref, wb_ref, wo_ref, o_ref):
    mixed = None
    for n, y_ref in enumerate((ya_ref, yb_ref, yc_ref, ym_ref)):
        up = _dot(y_ref[0], wb_ref[n])
        gate = _sigmoid(mg_ref[0, :, n * D_MODEL:(n + 1) * D_MODEL].astype(F32))
        mixed = gate * up if mixed is None else mixed + gate * up
    o_ref[0] = x_ref[0] + _dot(mixed.astype(BF16), wo_ref[...])


def _tail(x, ya, yb, yc, ym, mg, wb_bf16, wo_bf16):
    b, s, d = x.shape
    tm = TM_TAIL
    br = pl.BlockSpec((1, tm, BRANCH_WIDTH), lambda bi, i: (bi, i, 0))
    return pl.pallas_call(
        _tail_kernel,
        grid=(b, s // tm),
        in_specs=[
            pl.BlockSpec((1, tm, d), lambda bi, i: (bi, i, 0)),
            br, br, br, br,
            pl.BlockSpec((1, tm, N_BRANCH * d), lambda bi, i: (bi, i, 0)),
            pl.BlockSpec((N_BRANCH, BRANCH_WIDTH, d), lambda bi, i: (0, 0, 0)),
            pl.BlockSpec((d, d), lambda bi, i: (0, 0)),
        ],
        out_specs=pl.BlockSpec((1, tm, d), lambda bi, i: (bi, i, 0)),
        out_shape=jax.ShapeDtypeStruct((b, s, d), F32),
        compiler_params=pltpu.CompilerParams(
            dimension_semantics=("parallel", "parallel"), vmem_limit_bytes=VMEM_LIMIT),
        name="merge_out",
    )(x, ya, yb, yc, ym, mg, wb_bf16, wo_bf16)


def _head_mean_matrix(width, head_dim):
    idx = jnp.arange(width) // head_dim
    return jnp.where(idx[:, None] == idx[None, :], 1.0 / head_dim, 0.0).astype(BF16)


def _block_diag(w):
    n, c, d = w.shape
    eye = jnp.eye(n, dtype=w.dtype)
    return jnp.einsum('ncd,nm->ncmd', w, eye).reshape(n * c, n * d)


def _layer(x, mem, norm_gain, w_in, swa_q_gain, swa_k_gain, swa_sinks, conv_w, conv_b,
           lru_w_a, lru_b_a, lru_w_x, lru_b_x, lru_lambda, mem_norm_gain, w_mem_kv,
           mem_q_gain, mem_k_gain, w_branch, w_out, consts):
    bdq, bdk, tri = consts
    row = lambda v: v.reshape(1, -1).astype(F32)
    mkt, mv = _memkv(mem, row(mem_norm_gain), w_mem_kv.astype(BF16), row(mem_k_gain))
    qg = row(jnp.tile(swa_q_gain, SWA_HEADS) * (SWA_HEAD_DIM ** -0.5))
    kg = row(jnp.tile(swa_k_gain, SWA_KV_HEADS))
    ya, yc, ym, qb, kb, vb, gb, mg = _front(
        x, swa_sinks.reshape(1, SWA_HEADS).astype(F32), row(norm_gain), w_in.astype(BF16),
        qg, kg, bdq, bdk,
        conv_w.astype(F32), row(conv_b), _block_diag(lru_w_a).astype(BF16), row(lru_b_a),
        _block_diag(lru_w_x).astype(BF16), row(lru_b_x), row(lru_lambda),
        mkt, mv, row(mem_q_gain))
    yb = _sb(qb, kb, vb, gb, tri)
    return _tail(x, ya, yb, yc, ym, mg, w_branch.astype(BF16), w_out.astype(BF16))


def kernel(x, mem, norm_gain, w_in, swa_q_gain, swa_k_gain, swa_sinks, conv_w, conv_b, lru_w_a,
           lru_b_a, lru_w_x, lru_b_x, lru_lambda, mem_norm_gain, w_mem_kv, mem_q_gain,
           mem_k_gain, w_branch, w_out):
    depth = w_in.shape[0]
    consts = (
        _head_mean_matrix(SWA_HEADS * SWA_HEAD_DIM, SWA_HEAD_DIM),
        _head_mean_matrix(SWA_KV_HEADS * SWA_HEAD_DIM, SWA_HEAD_DIM),
        jnp.stack([jnp.tril(jnp.ones((TK_SB, TK_SB), F32), 0),
                   jnp.tril(jnp.ones((TK_SB, TK_SB), F32), -1)]).astype(BF16),
    )
    for l in range(depth):
        x = _layer(x, mem, norm_gain[l], w_in[l], swa_q_gain[l], swa_k_gain[l], swa_sinks[l],
                   conv_w[l], conv_b[l], lru_w_a[l], lru_b_a[l], lru_w_x[l], lru_b_x[l],
                   lru_lambda[l], mem_norm_gain[l], w_mem_kv[l], mem_q_gain[l], mem_k_gain[l],
                   w_branch[l], w_out[l], consts)
    return x
```

```python
import jax
import jax.numpy as jnp
from jax import lax
from jax.experimental import pallas as pl
from jax.experimental.pallas import tpu as pltpu

F32 = jnp.float32
BF16 = jnp.bfloat16

D_MODEL = 1024
BRANCH_WIDTH = 512
N_BRANCH = 4
SWA_HEAD_DIM = 64
SWA_HEADS = 8
SWA_KV_HEADS = 2
SWA_WINDOW = 128
SB_HEAD_DIM = 64
LRU_BLOCKS = 8
CONV_WIDTH = 4
LRU_C = 8.0
MEM_HEADS = 4
MEM_HEAD_DIM = 128
EPS = 1e-6

LANES = 128
SUBLANES = 8
HEAD_PAIRS = BRANCH_WIDTH // LANES

_SIZES = (512, 128, 128, 512, 512, 512, 512, 512, 512, 512, 512, 512, 4096)
_OFFS = tuple(sum(_SIZES[:i]) for i in range(len(_SIZES)))
(OFF_AQ, OFF_AK, OFF_AV, OFF_AG, OFF_BQ, OFF_BK, OFF_BV, OFF_BG,
 OFF_CX, OFF_CG, OFF_MQ, OFF_MG, OFF_MERGE) = _OFFS
IN_WIDTH = sum(_SIZES)

TM_BRANCH = 512
TM_MERGE = 512
TK_SB = 256
LRU_CHUNK = 64
FILL_COLS = 256

VMEM_LIMIT = 62 * 1024 * 1024

SB_EXIT_CARRY = 88.0


def _dot(a, b):
    return jnp.dot(a, b, preferred_element_type=F32)


def _dot_nt(a, b):
    return lax.dot_general(a, b, (((1,), (1,)), ((), ())), preferred_element_type=F32)


def _sigmoid(x):
    return 1.0 / (1.0 + jnp.exp(-x))


def _silu(x):
    return x * _sigmoid(x)


def _rms_scale(x):
    return lax.rsqrt(jnp.mean(x * x, axis=-1, keepdims=True) + EPS)


def _swa_branch(sink_ref, q, g, kcat, vcat, qg, kg, bdq, bdk, at_start, o_ref):
    tq, sub = q.shape[0], SWA_WINDOW
    nkeys = tq + sub
    msq = _dot((q * q).astype(BF16), bdq)
    qn = (q * lax.rsqrt(msq + EPS) * qg).astype(BF16)
    msk = _dot((kcat * kcat).astype(BF16), bdk)
    kn_t = (kcat * lax.rsqrt(msk + EPS) * kg).T
    zeros = jnp.zeros((SWA_HEAD_DIM, nkeys), F32)

    def k_variant(half, kv):
        blk = kn_t[kv * SWA_HEAD_DIM:(kv + 1) * SWA_HEAD_DIM]
        parts = [blk, zeros] if half == 0 else [zeros, blk]
        return jnp.concatenate(parts, axis=0).astype(BF16)

    kt = [[k_variant(half, kv) for kv in range(SWA_KV_HEADS)] for half in range(2)]
    v_same = vcat.astype(BF16)
    v_swap = pltpu.roll(vcat, SWA_HEAD_DIM, axis=1).astype(BF16)

    r = lax.broadcasted_iota(jnp.int32, (sub, 2 * sub), 0)
    j = lax.broadcasted_iota(jnp.int32, (sub, 2 * sub), 1)
    diff = r - (j - sub)
    band = (diff >= 0) & (diff < SWA_WINDOW)
    first = jnp.where(at_start, sub, 0)
    band_first = band & (j >= first)
    lane = lax.broadcasted_iota(jnp.int32, (sub, LANES), 1)
    row2 = lax.broadcasted_iota(jnp.int32, (2 * sub, 1), 0)

    for sb in range(tq // sub):
        rows = slice(sb * sub, (sb + 1) * sub)
        keys = slice(sb * sub, (sb + 2) * sub)
        valid = band_first if sb == 0 else band
        valid2 = jnp.concatenate([valid, valid], axis=0)
        outs = [None] * SWA_HEADS
        for kv in range(SWA_KV_HEADS):
            for half in range(2):
                pa, pb = 2 * kv, 2 * kv + 1
                ha, hb = 2 * pa + half, 2 * pb + half
                q2 = jnp.concatenate([qn[rows, pa * LANES:(pa + 1) * LANES],
                                      qn[rows, pb * LANES:(pb + 1) * LANES]], axis=0)
                sc = _dot(q2, kt[half][kv][:, keys])
                sc = jnp.where(valid2, sc, -jnp.inf)
                sink = jnp.where(row2 < sub, sink_ref[0, ha], sink_ref[0, hb])
                m = jnp.maximum(jnp.max(sc, axis=-1, keepdims=True), sink)
                pe = jnp.exp(sc - m)
                den = jnp.sum(pe, axis=-1, keepdims=True) + jnp.exp(sink - m)
                vv = v_same if half == kv else v_swap
                res = _dot(pe.astype(BF16), vv[keys]) * (1.0 / den)
                outs[ha], outs[hb] = res[:sub], res[sub:]
        blocks = [jnp.where(lane < SWA_HEAD_DIM, outs[2 * p], outs[2 * p + 1])
                  for p in range(HEAD_PAIRS)]
        y = jnp.concatenate(blocks, axis=1)
        o_ref[0, rows, :] = (y * _silu(g[rows])).astype(BF16)


def _lru_branch(x, g, cw_ref, cb_ref, wa_ref, ba_ref, wx_ref, bx_ref, lam_ref,
                xe_ref, h_ref, hs_ref, fill, never):
    tc = x.shape[0]
    halo = SUBLANES
    xe_ref[halo:halo + tc, :] = x
    xc = cb_ref[...] + cw_ref[CONV_WIDTH - 1:CONV_WIDTH, :] * x
    for tap in range(CONV_WIDTH - 1):
        back = CONV_WIDTH - 1 - tap
        xc = xc + cw_ref[tap:tap + 1, :] * xe_ref[halo - back:halo - back + tc, :]
    xe_ref[0:halo, :] = x[tc - halo:, :]

    xb = xc.astype(BF16)
    rg = _sigmoid(_dot(xb, wa_ref[...]) + ba_ref[...])
    ig = _sigmoid(_dot(xb, wx_ref[...]) + bx_ref[...])
    lam = lam_ref[...]
    softplus_neg_lam = jnp.maximum(-lam, 0.0) + jnp.log1p(jnp.exp(-jnp.abs(lam)))
    rate = -LRU_C * softplus_neg_lam

    groups = LRU_CHUNK // SUBLANES
    sub = lax.broadcasted_iota(jnp.int32, (groups, SUBLANES, BRANCH_WIDTH), 1)
    h = h_ref[...]
    for start in range(0, tc, LRU_CHUNK):
        rows = slice(start, start + LRU_CHUNK)
        a = jnp.exp(rg[rows] * rate)
        bv = jnp.sqrt(1.0 - a * a) * (ig[rows] * xc[rows])
        a3 = a.reshape(groups, SUBLANES, BRANCH_WIDTH)
        b3 = bv.reshape(groups, SUBLANES, BRANCH_WIDTH)
        shift = 1
        while shift < SUBLANES:
            keep = sub >= shift
            a_sh = pltpu.roll(a3, shift, axis=1)
            b_sh = pltpu.roll(b3, shift, axis=1)
            b3 = jnp.where(keep, a3 * b_sh + b3, b3)
            a3 = jnp.where(keep, a3 * a_sh, a3)
            shift *= 2
        for gi in range(groups):
            hg = a3[gi] * h + b3[gi]
            hs_ref[start + gi * SUBLANES:start + (gi + 1) * SUBLANES, :] = hg
            h = hg[SUBLANES - 1:SUBLANES, :]
        piece_row = fill()
        if piece_row is not None:
            rate = jnp.where(never, piece_row, rate)
    h_ref[...] = h
    return hs_ref[...] * _silu(g)


def _mem_branch(q, g, kt_ref, v_ref, qg):
    scale = MEM_HEAD_DIM ** -0.5
    outs = []
    for h in range(MEM_HEADS):
        sl = slice(h * MEM_HEAD_DIM, (h + 1) * MEM_HEAD_DIM)
        qh = q[:, sl]
        qn = (qh * _rms_scale(qh) * qg).astype(BF16)
        sc = _dot(qn, kt_ref[0, sl, :]) * scale
        m = jnp.max(sc, axis=-1, keepdims=True)
        pe = jnp.exp(sc - m)
        den = jnp.sum(pe, axis=-1, keepdims=True)
        outs.append(_dot(pe.astype(BF16), v_ref[0, :, sl]) * (1.0 / den))
    return jnp.concatenate(outs, axis=1) * _silu(g)


def _memkv_kernel(mem_ref, g_ref, w_ref, kg_ref, kt_ref, v_ref):
    m = mem_ref[0]
    u = (m * _rms_scale(m) * g_ref[...]).astype(BF16)
    kv = _dot(u, w_ref[...])
    for h in range(MEM_HEADS):
        kh = kv[:, h * MEM_HEAD_DIM:(h + 1) * MEM_HEAD_DIM]
        kn = kh * _rms_scale(kh) * kg_ref[...]
        kt_ref[0, h * MEM_HEAD_DIM:(h + 1) * MEM_HEAD_DIM, :] = kn.T.astype(BF16)
    v_ref[0] = kv[:, BRANCH_WIDTH:].astype(BF16)


def _memkv(mem, gain, w_bf16, kg):
    b, mlen, d = mem.shape
    return pl.pallas_call(
        _memkv_kernel,
        grid=(b,),
        in_specs=[
            pl.BlockSpec((1, mlen, d), lambda bi: (bi, 0, 0)),
            pl.BlockSpec((1, d), lambda bi: (0, 0)),
            pl.BlockSpec((d, 2 * BRANCH_WIDTH), lambda bi: (0, 0)),
            pl.BlockSpec((1, MEM_HEAD_DIM), lambda bi: (0, 0)),
        ],
        out_specs=[pl.BlockSpec((1, BRANCH_WIDTH, mlen), lambda bi: (bi, 0, 0)),
                   pl.BlockSpec((1, mlen, BRANCH_WIDTH), lambda bi: (bi, 0, 0))],
        out_shape=[jax.ShapeDtypeStruct((b, BRANCH_WIDTH, mlen), BF16),
                   jax.ShapeDtypeStruct((b, mlen, BRANCH_WIDTH), BF16)],
        compiler_params=pltpu.CompilerParams(
            dimension_semantics=("parallel",), vmem_limit_bytes=VMEM_LIMIT),
        name="memkv",
    )(mem, gain, w_bf16, kg)


def _sb_tile(q_half, kj, vj, tri_incl, carries, causal, causal_bf16):
    rows, tk = q_half[0].shape[0], kj.shape[0]
    outs, new_carries = [], []
    for a in range(2):
        z = _dot_nt(q_half[a], kj)
        zb = z.astype(BF16)
        sp = jnp.maximum(zb, 0) + jnp.log(1 + jnp.exp(-jnp.abs(zb)))
        if causal_bf16 is not None:
            sp = sp * causal_bf16
        cum = _dot(sp, tri_incl)
        carry = carries[a]
        arg = z - (cum + jnp.concatenate([carry] * (tk // LANES), axis=1))
        if causal is not None:
            arg = jnp.where(causal, arg, -jnp.inf)
        outs.append(_dot(jnp.exp(arg).astype(BF16), vj))
        new_carries.append(carry + jnp.broadcast_to(cum[:, 0:1], (rows, LANES)))
    return outs, new_carries


def _q_halves(q):
    lane = lax.broadcasted_iota(jnp.int32, q.shape, 1)
    zero = jnp.zeros_like(q)
    return [jnp.where(lane < SB_HEAD_DIM, q, zero), jnp.where(lane >= SB_HEAD_DIM, q, zero)]


def _branch_kernel(sink_ref, x_ref, g_ref, w_ref, qg_ref, kg_ref, bdq_ref, bdk_ref,
                   cw_ref, cb_ref, wa_ref, ba_ref, wx_ref, bx_ref, lam_ref,
                   mkt_ref, mv_ref, mqg_ref, tri_ref,
                   ya_ref, yb_ref, yc_ref, ym_ref,
                   kvp_ref, xe_ref, h_ref, hs_ref, gs_ref, qs_ref, kall_ref, vall_ref,
                   acc_ref, car_ref):
    i = pl.program_id(1)
    tm, tb = TM_BRANCH, TK_SB
    nsub = tm // tb

    @pl.when(i == 0)
    def _():
        kvp_ref[...] = jnp.zeros_like(kvp_ref)
        xe_ref[0:SUBLANES, :] = jnp.zeros((SUBLANES, BRANCH_WIDTH), F32)
        h_ref[...] = jnp.zeros_like(h_ref)

    x = x_ref[0]
    u = (x * _rms_scale(x) * g_ref[...]).astype(BF16)

    def proj(off, width):
        return _dot(u, w_ref[:, off:off + width])

    row0 = pl.multiple_of(i * tm, tm)

    def b_piece(kind, col):
        def emit():
            blocks = range(col // LANES, (col + FILL_COLS) // LANES)
            if kind == "g":
                res = proj(OFF_BG + col, FILL_COLS)
                gs_ref[:, col:col + FILL_COLS] = _silu(res)
            elif kind == "q":
                res = proj(OFF_BQ + col, FILL_COLS) * (SB_HEAD_DIM ** -0.5)
                val = res.astype(BF16)
                for n, p in enumerate(blocks):
                    qs_ref[p] = val[:, n * LANES:(n + 1) * LANES]
            else:
                off, ref = (OFF_BK, kall_ref) if kind == "k" else (OFF_BV, vall_ref)
                res = proj(off + col, FILL_COLS)
                val = res.astype(BF16)
                for n, p in enumerate(blocks):
                    ref[p, pl.ds(row0, tm), :] = val[:, n * LANES:(n + 1) * LANES]
            return res[0:1, :]
        return emit

    pending = iter([b_piece(kind, col) for kind in "qkvg"
                    for col in range(0, BRANCH_WIDTH, FILL_COLS)])

    def fill():
        emit = next(pending, None)
        if emit is None:
            return None
        row = emit()
        return jnp.concatenate([row] * (BRANCH_WIDTH // FILL_COLS), axis=1)

    yc = _lru_branch(proj(OFF_CX, 512), proj(OFF_CG, 512), cw_ref, cb_ref, wa_ref, ba_ref,
                     wx_ref, bx_ref, lam_ref, xe_ref, h_ref, hs_ref, fill, i < 0)
    yc_ref[0] = yc.astype(BF16)
    for emit in pending:
        emit()

    tri = tri_ref[0]
    strict = tri_ref[1]
    r = lax.broadcasted_iota(jnp.int32, (tb, tb), 0)
    c = lax.broadcasted_iota(jnp.int32, (tb, tb), 1)
    causal = c < r
    first = i * nsub

    def history_tile(p, jt):
        start = pl.multiple_of(jt * tb, tb)
        return kall_ref[p, pl.ds(start, tb), :], vall_ref[p, pl.ds(start, tb), :]

    def sb_pair(p):
        k0, v0 = history_tile(p, jnp.maximum(first - 1, 0))
        tiles = [(k0, jnp.where(first > 0, v0, jnp.zeros_like(v0)))]
        tiles += [history_tile(p, first + sb) for sb in range(nsub)]
        zeros = [jnp.zeros((tb, LANES), F32)] * 2
        for sb in range(nsub):
            qh = _q_halves(qs_ref[p, sb * tb:(sb + 1) * tb, :])
            out_d, car_d = _sb_tile(qh, *tiles[sb + 1], tri, zeros, causal, strict)
            out_l, car_l = _sb_tile(qh, *tiles[sb], tri, car_d, None, None)
            for a in range(2):
                acc_ref[p, sb, a] = out_d[a] + out_l[a]
                car_ref[p, sb, a] = car_l[a]

    sb_pair(0)
    sb_pair(1)

    kva = proj(OFF_AK, 2 * LANES)
    kvp = kvp_ref[...]
    kcat = jnp.concatenate([kvp[:, :LANES], kva[:, :LANES]], axis=0)
    vcat = jnp.concatenate([kvp[:, LANES:], kva[:, LANES:]], axis=0)
    kvp_ref[...] = kva[tm - SWA_WINDOW:, :]
    _swa_branch(sink_ref, proj(OFF_AQ, 512), proj(OFF_AG, 512), kcat, vcat, qg_ref[...],
                kg_ref[...], bdq_ref[...], bdk_ref[...], i == 0, ya_ref)

    sb_pair(2)
    sb_pair(3)

    ym = _mem_branch(proj(OFF_MQ, 512), proj(OFF_MG, 512), mkt_ref, mv_ref, mqg_ref[...])
    ym_ref[0] = ym.astype(BF16)

    def continue_left(p, sb):
        qh = _q_halves(qs_ref[p, pl.ds(pl.multiple_of(sb * tb, tb), tb), :])

        def not_done():
            return (jnp.min(car_ref[p, sb]) < SB_EXIT_CARRY).astype(jnp.int32)

        def cond(state):
            jt, live = state
            return jnp.logical_and(jt >= 0, live > 0)

        def body(state):
            jt, _ = state
            kj, vj = history_tile(p, jt)
            outs, cars = _sb_tile(qh, kj, vj, tri, [car_ref[p, sb, 0], car_ref[p, sb, 1]],
                                  None, None)
            for a in range(2):
                acc_ref[p, sb, a] += outs[a]
                car_ref[p, sb, a] = cars[a]
            return jt - 1, not_done()

        lax.while_loop(cond, body, (first + sb - 2, not_done()))

    @pl.when(jnp.min(car_ref[...]) < SB_EXIT_CARRY)
    def _():
        def over_pairs(p, unused):
            def over_subs(sb, unused2):
                continue_left(p, sb)
                return unused2
            return lax.fori_loop(0, nsub, over_subs, unused)
        lax.fori_loop(0, HEAD_PAIRS, over_pairs, 0)

    lane = lax.broadcasted_iota(jnp.int32, (tb, LANES), 1)
    for p in range(HEAD_PAIRS):
        for sb in range(nsub):
            rows = slice(sb * tb, (sb + 1) * tb)
            y = jnp.where(lane < SB_HEAD_DIM, acc_ref[p, sb, 0], acc_ref[p, sb, 1])
            yb_ref[0, rows, p * LANES:(p + 1) * LANES] = (
                y * gs_ref[rows, p * LANES:(p + 1) * LANES]).astype(BF16)


def _branches(x, sinks, gain, w_bf16, qg, kg, bdq, bdk, cw, cb, wa, ba, wx, bx, lam, mkt, mv,
              mqg, tri):
    b, s, d = x.shape
    w = BRANCH_WIDTH
    mlen = mv.shape[1]
    tm, tb = TM_BRANCH, TK_SB
    const = lambda *shape: pl.BlockSpec(shape, lambda bi, i: (0,) * len(shape),
                                        pipeline_mode=pl.Buffered(1))
    row = pl.BlockSpec((1, tm, w), lambda bi, i: (bi, i, 0))
    out = jax.ShapeDtypeStruct((b, s, w), BF16)
    return pl.pallas_call(
        _branch_kernel,
        grid=(b, s // tm),
        in_specs=[
            pl.BlockSpec(memory_space=pltpu.SMEM),
            pl.BlockSpec((1, tm, d), lambda bi, i: (bi, i, 0)),
            const(1, d),
            pl.BlockSpec((d, OFF_MERGE), lambda bi, i: (0, 0), pipeline_mode=pl.Buffered(1)),
            const(1, w), const(1, LANES), const(w, w), const(LANES, LANES),
            const(CONV_WIDTH, w), const(1, w), const(w, w), const(1, w), const(w, w),
            const(1, w), const(1, w),
            pl.BlockSpec((1, w, mlen), lambda bi, i: (bi, 0, 0)),
            pl.BlockSpec((1, mlen, w), lambda bi, i: (bi, 0, 0)),
            const(1, MEM_HEAD_DIM),
            const(2, tb, tb),
        ],
        out_specs=[row, row, row, row],
        out_shape=[out, out, out, out],
        scratch_shapes=[
            pltpu.VMEM((SWA_WINDOW, 2 * LANES), F32),
            pltpu.VMEM((tm + SUBLANES, w), F32),
            pltpu.VMEM((1, w), F32),
            pltpu.VMEM((tm, w), F32),
            pltpu.VMEM((tm, w), F32),
            pltpu.VMEM((HEAD_PAIRS, tm, LANES), BF16),
            pltpu.VMEM((HEAD_PAIRS, s, LANES), BF16),
            pltpu.VMEM((HEAD_PAIRS, s, LANES), BF16),
            pltpu.VMEM((HEAD_PAIRS, tm // tb, 2, tb, LANES), F32),
            pltpu.VMEM((HEAD_PAIRS, tm // tb, 2, tb, LANES), F32),
        ],
        compiler_params=pltpu.CompilerParams(
            dimension_semantics=("parallel", "arbitrary"), vmem_limit_bytes=VMEM_LIMIT),
        name="branches",
    )(sinks, x, gain, w_bf16, qg, kg, bdq, bdk, cw, cb, wa, ba, wx, bx, lam, mkt, mv, mqg, tri)


def _merge_kernel(x_ref, g_ref, wm_ref, ya_ref, yb_ref, yc_ref, ym_ref, wb_ref, wo_ref, o_ref):
    x = x_ref[0]
    u = (x * _rms_scale(x) * g_ref[...]).astype(BF16)
    mixed = None
    for n, y_ref in enumerate((ya_ref, yb_ref, yc_ref, ym_ref)):
        up = _dot(y_ref[0], wb_ref[n])
        gate = _sigmoid(_dot(u, wm_ref[:, n * D_MODEL:(n + 1) * D_MODEL]))
        mixed = gate * up if mixed is None else mixed + gate * up
    o_ref[0] = x + _dot(mixed.astype(BF16), wo_ref[...])


def _merge(x, gain, wm_bf16, ya, yb, yc, ym, wb_bf16, wo_bf16):
    b, s, d = x.shape
    tm = TM_MERGE
    br = pl.BlockSpec((1, tm, BRANCH_WIDTH), lambda bi, i: (bi, i, 0))
    return pl.pallas_call(
        _merge_kernel,
        grid=(b, s // tm),
        in_specs=[
            pl.BlockSpec((1, tm, d), lambda bi, i: (bi, i, 0)),
            pl.BlockSpec((1, d), lambda bi, i: (0, 0)),
            pl.BlockSpec((d, N_BRANCH * d), lambda bi, i: (0, 0), pipeline_mode=pl.Buffered(1)),
            br, br, br, br,
            pl.BlockSpec((N_BRANCH, BRANCH_WIDTH, d), lambda bi, i: (0, 0, 0)),
            pl.BlockSpec((d, d), lambda bi, i: (0, 0)),
        ],
        out_specs=pl.BlockSpec((1, tm, d), lambda bi, i: (bi, i, 0)),
        out_shape=jax.ShapeDtypeStruct((b, s, d), F32),
        compiler_params=pltpu.CompilerParams(
            dimension_semantics=("parallel", "parallel"), vmem_limit_bytes=VMEM_LIMIT),
        name="merge_out",
    )(x, gain, wm_bf16, ya, yb, yc, ym, wb_bf16, wo_bf16)


def _head_mean_matrix(width, head_dim):
    idx = jnp.arange(width) // head_dim
    return jnp.where(idx[:, None] == idx[None, :], 1.0 / head_dim, 0.0).astype(BF16)


def _block_diag(w):
    n, c, d = w.shape
    eye = jnp.eye(n, dtype=w.dtype)
    return jnp.einsum('ncd,nm->ncmd', w, eye).reshape(n * c, n * d)


def _layer(x, mem, norm_gain, w_in, swa_q_gain, swa_k_gain, swa_sinks, conv_w, conv_b,
           lru_w_a, lru_b_a, lru_w_x, lru_b_x, lru_lambda, mem_norm_gain, w_mem_kv,
           mem_q_gain, mem_k_gain, w_branch, w_out, consts):
    bdq, bdk, tri = consts
    row = lambda v: v.reshape(1, -1).astype(F32)
    mkt, mv = _memkv(mem, row(mem_norm_gain), w_mem_kv.astype(BF16), row(mem_k_gain))
    qg = row(jnp.tile(swa_q_gain, SWA_HEADS) * (SWA_HEAD_DIM ** -0.5))
    kg = row(jnp.tile(swa_k_gain, SWA_KV_HEADS))
    w_bf16 = w_in.astype(BF16)
    ya, yb, yc, ym = _branches(
        x, swa_sinks.reshape(1, SWA_HEADS).astype(F32), row(norm_gain), w_bf16[:, :OFF_MERGE],
        qg, kg, bdq, bdk,
        conv_w.astype(F32), row(conv_b), _block_diag(lru_w_a).astype(BF16), row(lru_b_a),
        _block_diag(lru_w_x).astype(BF16), row(lru_b_x), row(lru_lambda),
        mkt, mv, row(mem_q_gain), tri)
    return _merge(x, row(norm_gain), w_bf16[:, OFF_MERGE:], ya, yb, yc, ym,
                  w_branch.astype(BF16), w_out.astype(BF16))


def kernel(x, mem, norm_gain, w_in, swa_q_gain, swa_k_gain, swa_sinks, conv_w, conv_b, lru_w_a,
           lru_b_a, lru_w_x, lru_b_x, lru_lambda, mem_norm_gain, w_mem_kv, mem_q_gain,
           mem_k_gain, w_branch, w_out):
    depth = w_in.shape[0]
    consts = (
        _head_mean_matrix(SWA_HEADS * SWA_HEAD_DIM, SWA_HEAD_DIM),
        _head_mean_matrix(SWA_KV_HEADS * SWA_HEAD_DIM, SWA_HEAD_DIM),
        jnp.stack([jnp.tril(jnp.ones((TK_SB, TK_SB), F32), 0),
                   jnp.tril(jnp.ones((TK_SB, TK_SB), F32), -1)]).astype(BF16),
    )
    for l in range(depth):
        x = _layer(x, mem, norm_gain[l], w_in[l], swa_q_gain[l], swa_k_gain[l], swa_sinks[l],
                   conv_w[l], conv_b[l], lru_w_a[l], lru_b_a[l], lru_w_x[l], lru_b_x[l],
                   lru_lambda[l], mem_norm_gain[l], w_mem_kv[l], mem_q_gain[l], mem_k_gain[l],
                   w_branch[l], w_out[l], consts)
    return x
```

```python
import jax
import jax.numpy as jnp
from jax import lax
from jax.experimental import pallas as pl
from jax.experimental.pallas import tpu as pltpu

F32 = jnp.float32
BF16 = jnp.bfloat16

D_MODEL = 1024
BRANCH_WIDTH = 512
N_BRANCH = 4
SWA_HEAD_DIM = 64
SWA_HEADS = 8
SWA_KV_HEADS = 2
SWA_WINDOW = 128
SB_HEAD_DIM = 64
LRU_BLOCKS = 8
CONV_WIDTH = 4
LRU_C = 8.0
MEM_HEADS = 4
MEM_HEAD_DIM = 128
EPS = 1e-6

LANES = 128
SUBLANES = 8
HEAD_PAIRS = BRANCH_WIDTH // LANES

_SIZES = (512, 128, 128, 512, 512, 512, 512, 512, 512, 512, 512, 512, 4096)
_OFFS = tuple(sum(_SIZES[:i]) for i in range(len(_SIZES)))
(OFF_AQ, OFF_AK, OFF_AV, OFF_AG, OFF_BQ, OFF_BK, OFF_BV, OFF_BG,
 OFF_CX, OFF_CG, OFF_MQ, OFF_MG, OFF_MERGE) = _OFFS
IN_WIDTH = sum(_SIZES)

TM_BRANCH = 512
TM_MERGE = 1024
TK_SB = 256
LRU_CHUNK = 64
FILL_COLS = 256

VMEM_LIMIT = 62 * 1024 * 1024

SB_EXIT_CARRY = 88.0


def _dot(a, b):
    return jnp.dot(a, b, preferred_element_type=F32)


def _dot_nt(a, b):
    return lax.dot_general(a, b, (((1,), (1,)), ((), ())), preferred_element_type=F32)


def _sigmoid(x):
    return 1.0 / (1.0 + jnp.exp(-x))


def _silu(x):
    return x * _sigmoid(x)


def _rms_scale(x):
    return lax.rsqrt(jnp.mean(x * x, axis=-1, keepdims=True) + EPS)


def _swa_branch(sink_ref, q, g, kcat, vcat, qg, kg, bdq, bdk, at_start, o_ref):
    tq, sub = q.shape[0], SWA_WINDOW
    nkeys = tq + sub
    msq = _dot((q * q).astype(BF16), bdq)
    qn = (q * lax.rsqrt(msq + EPS) * qg).astype(BF16)
    msk = _dot((kcat * kcat).astype(BF16), bdk)
    kn_t = (kcat * lax.rsqrt(msk + EPS) * kg).T
    zeros = jnp.zeros((SWA_HEAD_DIM, nkeys), F32)

    def k_variant(half, kv):
        blk = kn_t[kv * SWA_HEAD_DIM:(kv + 1) * SWA_HEAD_DIM]
        parts = [blk, zeros] if half == 0 else [zeros, blk]
        return jnp.concatenate(parts, axis=0).astype(BF16)

    kt = [[k_variant(half, kv) for kv in range(SWA_KV_HEADS)] for half in range(2)]
    v_same = vcat.astype(BF16)
    v_swap = pltpu.roll(vcat, SWA_HEAD_DIM, axis=1).astype(BF16)

    r = lax.broadcasted_iota(jnp.int32, (sub, 2 * sub), 0)
    j = lax.broadcasted_iota(jnp.int32, (sub, 2 * sub), 1)
    diff = r - (j - sub)
    band = (diff >= 0) & (diff < SWA_WINDOW)
    first = jnp.where(at_start, sub, 0)
    band_first = band & (j >= first)
    lane = lax.broadcasted_iota(jnp.int32, (sub, LANES), 1)
    row2 = lax.broadcasted_iota(jnp.int32, (2 * sub, 1), 0)

    for sb in range(tq // sub):
        rows = slice(sb * sub, (sb + 1) * sub)
        keys = slice(sb * sub, (sb + 2) * sub)
        valid = band_first if sb == 0 else band
        valid2 = jnp.concatenate([valid, valid], axis=0)
        outs = [None] * SWA_HEADS
        for kv in range(SWA_KV_HEADS):
            for half in range(2):
                pa, pb = 2 * kv, 2 * kv + 1
                ha, hb = 2 * pa + half, 2 * pb + half
                q2 = jnp.concatenate([qn[rows, pa * LANES:(pa + 1) * LANES],
                                      qn[rows, pb * LANES:(pb + 1) * LANES]], axis=0)
                sc = _dot(q2, kt[half][kv][:, keys])
                sc = jnp.where(valid2, sc, -jnp.inf)
                sink = jnp.where(row2 < sub, sink_ref[0, ha], sink_ref[0, hb])
                m = jnp.maximum(jnp.max(sc, axis=-1, keepdims=True), sink)
                pe = jnp.exp(sc - m)
                den = jnp.sum(pe, axis=-1, keepdims=True) + jnp.exp(sink - m)
                vv = v_same if half == kv else v_swap
                res = _dot(pe.astype(BF16), vv[keys]) * (1.0 / den)
                outs[ha], outs[hb] = res[:sub], res[sub:]
        blocks = [jnp.where(lane < SWA_HEAD_DIM, outs[2 * p], outs[2 * p + 1])
                  for p in range(HEAD_PAIRS)]
        y = jnp.concatenate(blocks, axis=1)
        o_ref[0, rows, :] = (y * _silu(g[rows])).astype(BF16)


def _lru_branch(x, g, cw_ref, cb_ref, wa_ref, ba_ref, wx_ref, bx_ref, lam_ref,
                xe_ref, h_ref, hs_ref, fill, never):
    tc = x.shape[0]
    halo = SUBLANES
    xe_ref[halo:halo + tc, :] = x
    xc = cb_ref[...] + cw_ref[CONV_WIDTH - 1:CONV_WIDTH, :] * x
    for tap in range(CONV_WIDTH - 1):
        back = CONV_WIDTH - 1 - tap
        xc = xc + cw_ref[tap:tap + 1, :] * xe_ref[halo - back:halo - back + tc, :]
    xe_ref[0:halo, :] = x[tc - halo:, :]

    xb = xc.astype(BF16)
    rg = _sigmoid(_dot(xb, wa_ref[...]) + ba_ref[...])
    ig = _sigmoid(_dot(xb, wx_ref[...]) + bx_ref[...])
    lam = lam_ref[...]
    softplus_neg_lam = jnp.maximum(-lam, 0.0) + jnp.log1p(jnp.exp(-jnp.abs(lam)))
    rate = -LRU_C * softplus_neg_lam

    groups = LRU_CHUNK // SUBLANES
    sub = lax.broadcasted_iota(jnp.int32, (groups, SUBLANES, BRANCH_WIDTH), 1)
    h = h_ref[...]
    for start in range(0, tc, LRU_CHUNK):
        rows = slice(start, start + LRU_CHUNK)
        a = jnp.exp(rg[rows] * rate)
        bv = jnp.sqrt(1.0 - a * a) * (ig[rows] * xc[rows])
        a3 = a.reshape(groups, SUBLANES, BRANCH_WIDTH)
        b3 = bv.reshape(groups, SUBLANES, BRANCH_WIDTH)
        shift = 1
        while shift < SUBLANES:
            keep = sub >= shift
            a_sh = pltpu.roll(a3, shift, axis=1)
            b_sh = pltpu.roll(b3, shift, axis=1)
            b3 = jnp.where(keep, a3 * b_sh + b3, b3)
            a3 = jnp.where(keep, a3 * a_sh, a3)
            shift *= 2
        for gi in range(groups):
            hg = a3[gi] * h + b3[gi]
            hs_ref[start + gi * SUBLANES:start + (gi + 1) * SUBLANES, :] = hg
            h = hg[SUBLANES - 1:SUBLANES, :]
        piece_row = fill()
        if piece_row is not None:
            rate = jnp.where(never, piece_row, rate)
    h_ref[...] = h
    return hs_ref[...] * _silu(g)


def _mem_branch(q, g, kt_ref, v_ref, qg):
    scale = MEM_HEAD_DIM ** -0.5
    outs = []
    for h in range(MEM_HEADS):
        sl = slice(h * MEM_HEAD_DIM, (h + 1) * MEM_HEAD_DIM)
        qh = q[:, sl]
        qn = (qh * _rms_scale(qh) * qg).astype(BF16)
        sc = _dot(qn, kt_ref[0, sl, :]) * scale
        m = jnp.max(sc, axis=-1, keepdims=True)
        pe = jnp.exp(sc - m)
        den = jnp.sum(pe, axis=-1, keepdims=True)
        outs.append(_dot(pe.astype(BF16), v_ref[0, :, sl]) * (1.0 / den))
    return jnp.concatenate(outs, axis=1) * _silu(g)


def _memkv_kernel(mem_ref, g_ref, w_ref, kg_ref, kt_ref, v_ref):
    m = mem_ref[0]
    u = (m * _rms_scale(m) * g_ref[...]).astype(BF16)
    kv = _dot(u, w_ref[...])
    for h in range(MEM_HEADS):
        kh = kv[:, h * MEM_HEAD_DIM:(h + 1) * MEM_HEAD_DIM]
        kn = kh * _rms_scale(kh) * kg_ref[...]
        kt_ref[0, h * MEM_HEAD_DIM:(h + 1) * MEM_HEAD_DIM, :] = kn.T.astype(BF16)
    v_ref[0] = kv[:, BRANCH_WIDTH:].astype(BF16)


def _memkv(mem, gain, w_all, kg, layer):
    b, mlen, d = mem.shape
    return pl.pallas_call(
        _memkv_kernel,
        grid=(b,),
        in_specs=[
            pl.BlockSpec((1, mlen, d), lambda bi: (bi, 0, 0)),
            pl.BlockSpec((1, d), lambda bi: (0, 0)),
            pl.BlockSpec((None, d, 2 * BRANCH_WIDTH), lambda bi: (layer, 0, 0)),
            pl.BlockSpec((1, MEM_HEAD_DIM), lambda bi: (0, 0)),
        ],
        out_specs=[pl.BlockSpec((1, BRANCH_WIDTH, mlen), lambda bi: (bi, 0, 0)),
                   pl.BlockSpec((1, mlen, BRANCH_WIDTH), lambda bi: (bi, 0, 0))],
        out_shape=[jax.ShapeDtypeStruct((b, BRANCH_WIDTH, mlen), BF16),
                   jax.ShapeDtypeStruct((b, mlen, BRANCH_WIDTH), BF16)],
        compiler_params=pltpu.CompilerParams(
            dimension_semantics=("parallel",), vmem_limit_bytes=VMEM_LIMIT),
        name="memkv",
    )(mem, gain, w_all, kg)


def _sb_tile(q_half, kj, vj, tri_incl, carries, causal, causal_bf16):
    rows, tk = q_half[0].shape[0], kj.shape[0]
    outs, new_carries = [], []
    for a in range(2):
        z = _dot_nt(q_half[a], kj)
        zb = z.astype(BF16)
        sp = jnp.maximum(zb, 0) + jnp.log(1 + jnp.exp(-jnp.abs(zb)))
        if causal_bf16 is not None:
            sp = sp * causal_bf16
        cum = _dot(sp, tri_incl)
        carry = carries[a]
        arg = z - (cum + jnp.concatenate([carry] * (tk // LANES), axis=1))
        if causal is not None:
            arg = jnp.where(causal, arg, -jnp.inf)
        outs.append(_dot(jnp.exp(arg).astype(BF16), vj))
        new_carries.append(carry + jnp.broadcast_to(cum[:, 0:1], (rows, LANES)))
    return outs, new_carries


def _q_halves(q):
    lane = lax.broadcasted_iota(jnp.int32, q.shape, 1)
    zero = jnp.zeros_like(q)
    return [jnp.where(lane < SB_HEAD_DIM, q, zero), jnp.where(lane >= SB_HEAD_DIM, q, zero)]


def _branch_kernel(sink_ref, x_ref, g_ref, w_ref, qg_ref, kg_ref, bdq_ref, bdk_ref,
                   cw_ref, cb_ref, wa_ref, ba_ref, wx_ref, bx_ref, lam_ref,
                   mkt_ref, mv_ref, mqg_ref, tri_ref,
                   ya_ref, yb_ref, yc_ref, ym_ref,
                   kvp_ref, xe_ref, h_ref, hs_ref, gs_ref, qs_ref, kall_ref, vall_ref,
                   acc_ref, car_ref):
    i = pl.program_id(1)
    tm, tb = TM_BRANCH, TK_SB
    nsub = tm // tb
    never = i < 0

    @pl.when(i == 0)
    def _():
        kvp_ref[...] = jnp.zeros_like(kvp_ref)
        xe_ref[0:SUBLANES, :] = jnp.zeros((SUBLANES, BRANCH_WIDTH), F32)
        h_ref[...] = jnp.zeros_like(h_ref)

    x = x_ref[0]
    u = (x * _rms_scale(x) * g_ref[...]).astype(BF16)

    def proj(off, width):
        return _dot(u, w_ref[:, off:off + width])

    row0 = pl.multiple_of(i * tm, tm)

    def b_piece(kind, col):
        def emit():
            blocks = range(col // LANES, (col + FILL_COLS) // LANES)
            if kind == "g":
                res = proj(OFF_BG + col, FILL_COLS)
                gs_ref[:, col:col + FILL_COLS] = _silu(res)
            elif kind == "q":
                res = proj(OFF_BQ + col, FILL_COLS) * (SB_HEAD_DIM ** -0.5)
                val = res.astype(BF16)
                for n, p in enumerate(blocks):
                    qs_ref[p] = val[:, n * LANES:(n + 1) * LANES]
            else:
                off, ref = (OFF_BK, kall_ref) if kind == "k" else (OFF_BV, vall_ref)
                res = proj(off + col, FILL_COLS)
                val = res.astype(BF16)
                for n, p in enumerate(blocks):
                    ref[p, pl.ds(row0, tm), :] = val[:, n * LANES:(n + 1) * LANES]
            return res[0:1, :]
        return emit

    pending = iter([b_piece(kind, col) for kind in "qkvg"
                    for col in range(0, BRANCH_WIDTH, FILL_COLS)])

    def fill():
        emit = next(pending, None)
        if emit is None:
            return None
        row = emit()
        return jnp.concatenate([row] * (BRANCH_WIDTH // FILL_COLS), axis=1)

    yc = _lru_branch(proj(OFF_CX, 512), proj(OFF_CG, 512), cw_ref, cb_ref, wa_ref, ba_ref,
                     wx_ref, bx_ref, lam_ref, xe_ref, h_ref, hs_ref, fill, never)
    yc_ref[0] = yc.astype(BF16)
    for emit in pending:
        emit()

    tri = tri_ref[0]
    strict = tri_ref[1]
    r = lax.broadcasted_iota(jnp.int32, (tb, tb), 0)
    c = lax.broadcasted_iota(jnp.int32, (tb, tb), 1)
    causal = c < r
    first = i * nsub

    def history_tile(p, jt):
        start = pl.multiple_of(jt * tb, tb)
        return kall_ref[p, pl.ds(start, tb), :], vall_ref[p, pl.ds(start, tb), :]

    def sb_pair(p):
        k0, v0 = history_tile(p, jnp.maximum(first - 1, 0))
        tiles = [(k0, jnp.where(first > 0, v0, jnp.zeros_like(v0)))]
        tiles += [history_tile(p, first + sb) for sb in range(nsub)]
        zeros = [jnp.zeros((tb, LANES), F32)] * 2
        for sb in range(nsub):
            qh = _q_halves(qs_ref[p, sb * tb:(sb + 1) * tb, :])
            out_d, car_d = _sb_tile(qh, *tiles[sb + 1], tri, zeros, causal, strict)
            out_l, car_l = _sb_tile(qh, *tiles[sb], tri, car_d, None, None)
            for a in range(2):
                acc_ref[p, sb, a] = out_d[a] + out_l[a]
                car_ref[p, sb, a] = car_l[a]

    for p in range(HEAD_PAIRS):
        sb_pair(p)

    kva = proj(OFF_AK, 2 * LANES)
    kvp = kvp_ref[...]
    kcat = jnp.concatenate([kvp[:, :LANES], kva[:, :LANES]], axis=0)
    vcat = jnp.concatenate([kvp[:, LANES:], kva[:, LANES:]], axis=0)
    kvp_ref[...] = kva[tm - SWA_WINDOW:, :]
    _swa_branch(sink_ref, proj(OFF_AQ, 512), proj(OFF_AG, 512), kcat, vcat, qg_ref[...],
                kg_ref[...], bdq_ref[...], bdk_ref[...], i == 0, ya_ref)

    ym = _mem_branch(proj(OFF_MQ, 512), proj(OFF_MG, 512), mkt_ref, mv_ref, mqg_ref[...])
    ym_ref[0] = ym.astype(BF16)

    def continue_left(p, sb):
        qh = _q_halves(qs_ref[p, pl.ds(pl.multiple_of(sb * tb, tb), tb), :])

        def not_done():
            return (jnp.min(car_ref[p, sb]) < SB_EXIT_CARRY).astype(jnp.int32)

        def cond(state):
            jt, live = state
            return jnp.logical_and(jt >= 0, live > 0)

        def body(state):
            jt, _ = state
            kj, vj = history_tile(p, jt)
            outs, cars = _sb_tile(qh, kj, vj, tri, [car_ref[p, sb, 0], car_ref[p, sb, 1]],
                                  None, None)
            for a in range(2):
                acc_ref[p, sb, a] += outs[a]
                car_ref[p, sb, a] = cars[a]
            return jt - 1, not_done()

        lax.while_loop(cond, body, (first + sb - 2, not_done()))

    @pl.when(jnp.min(car_ref[...]) < SB_EXIT_CARRY)
    def _():
        def over_pairs(p, unused):
            def over_subs(sb, unused2):
                continue_left(p, sb)
                return unused2
            return lax.fori_loop(0, nsub, over_subs, unused)
        lax.fori_loop(0, HEAD_PAIRS, over_pairs, 0)

    lane = lax.broadcasted_iota(jnp.int32, (tb, LANES), 1)
    for p in range(HEAD_PAIRS):
        for sb in range(nsub):
            rows = slice(sb * tb, (sb + 1) * tb)
            y = jnp.where(lane < SB_HEAD_DIM, acc_ref[p, sb, 0], acc_ref[p, sb, 1])
            yb_ref[0, rows, p * LANES:(p + 1) * LANES] = (
                y * gs_ref[rows, p * LANES:(p + 1) * LANES]).astype(BF16)


def _branches(x, sinks, gain, w_all, layer, qg, kg, bdq, bdk, cw, cb, wa, ba, wx, bx, lam, mkt,
              mv, mqg, tri):
    b, s, d = x.shape
    w = BRANCH_WIDTH
    mlen = mv.shape[1]
    tm, tb = TM_BRANCH, TK_SB
    const = lambda *shape: pl.BlockSpec(shape, lambda bi, i: (0,) * len(shape),
                                        pipeline_mode=pl.Buffered(1))
    row = pl.BlockSpec((1, tm, w), lambda bi, i: (bi, i, 0))
    out = jax.ShapeDtypeStruct((b, s, w), BF16)
    return pl.pallas_call(
        _branch_kernel,
        grid=(b, s // tm),
        in_specs=[
            pl.BlockSpec(memory_space=pltpu.SMEM),
            pl.BlockSpec((1, tm, d), lambda bi, i: (bi, i, 0)),
            const(1, d),
            pl.BlockSpec((None, d, OFF_MERGE), lambda bi, i: (layer, 0, 1),
                         pipeline_mode=pl.Buffered(1)),
            const(1, w), const(1, LANES), const(w, w), const(LANES, LANES),
            const(CONV_WIDTH, w), const(1, w), const(w, w), const(1, w), const(w, w),
            const(1, w), const(1, w),
            pl.BlockSpec((1, w, mlen), lambda bi, i: (bi, 0, 0)),
            pl.BlockSpec((1, mlen, w), lambda bi, i: (bi, 0, 0)),
            const(1, MEM_HEAD_DIM),
            const(2, tb, tb),
        ],
        out_specs=[row, row, row, row],
        out_shape=[out, out, out, out],
        scratch_shapes=[
            pltpu.VMEM((SWA_WINDOW, 2 * LANES), F32),
            pltpu.VMEM((tm + SUBLANES, w), F32),
            pltpu.VMEM((1, w), F32),
            pltpu.VMEM((tm, w), F32),
            pltpu.VMEM((tm, w), F32),
            pltpu.VMEM((HEAD_PAIRS, tm, LANES), BF16),
            pltpu.VMEM((HEAD_PAIRS, s, LANES), BF16),
            pltpu.VMEM((HEAD_PAIRS, s, LANES), BF16),
            pltpu.VMEM((HEAD_PAIRS, tm // tb, 2, tb, LANES), F32),
            pltpu.VMEM((HEAD_PAIRS, tm // tb, 2, tb, LANES), F32),
        ],
        compiler_params=pltpu.CompilerParams(
            dimension_semantics=("parallel", "arbitrary"), vmem_limit_bytes=VMEM_LIMIT),
        name="branches",
    )(sinks, x, gain, w_all, qg, kg, bdq, bdk, cw, cb, wa, ba, wx, bx, lam, mkt, mv, mqg, tri)


def _merge_kernel(x_ref, g_ref, wm_ref, ya_ref, yb_ref, yc_ref, ym_ref, wb_ref, wo_ref, o_ref):
    x = x_ref[0]
    u = (x * _rms_scale(x) * g_ref[...]).astype(BF16)
    mixed = None
    for n, y_ref in enumerate((ya_ref, yb_ref, yc_ref, ym_ref)):
        up = _dot(y_ref[0], wb_ref[n])
        gate = _sigmoid(_dot(u, wm_ref[:, n * D_MODEL:(n + 1) * D_MODEL]))
        mixed = gate * up if mixed is None else mixed + gate * up
    o_ref[0] = x + _dot(mixed.astype(BF16), wo_ref[...])


def _merge(x, gain, w_all, ya, yb, yc, ym, wb_all, wo_all, layer):
    b, s, d = x.shape
    tm = TM_MERGE
    br = pl.BlockSpec((1, tm, BRANCH_WIDTH), lambda bi, i: (bi, i, 0))
    return pl.pallas_call(
        _merge_kernel,
        grid=(b, s // tm),
        in_specs=[
            pl.BlockSpec((1, tm, d), lambda bi, i: (bi, i, 0)),
            pl.BlockSpec((1, d), lambda bi, i: (0, 0)),
            pl.BlockSpec((None, d, N_BRANCH * d), lambda bi, i: (layer, 0, 0),
                         pipeline_mode=pl.Buffered(1)),
            br, br, br, br,
            pl.BlockSpec((None, N_BRANCH, BRANCH_WIDTH, d), lambda bi, i: (layer, 0, 0, 0)),
            pl.BlockSpec((None, d, d), lambda bi, i: (layer, 0, 0)),
        ],
        out_specs=pl.BlockSpec((1, tm, d), lambda bi, i: (bi, i, 0)),
        out_shape=jax.ShapeDtypeStruct((b, s, d), F32),
        compiler_params=pltpu.CompilerParams(
            dimension_semantics=("parallel", "parallel"), vmem_limit_bytes=VMEM_LIMIT),
        name="merge_out",
    )(x, gain, w_all, ya, yb, yc, ym, wb_all, wo_all)


def _head_mean_matrix(width, head_dim):
    idx = jnp.arange(width) // head_dim
    return jnp.where(idx[:, None] == idx[None, :], 1.0 / head_dim, 0.0).astype(BF16)


def _block_diag(w):
    n, c, d = w.shape
    eye = jnp.eye(n, dtype=w.dtype)
    return jnp.einsum('ncd,nm->ncmd', w, eye).reshape(n * c, n * d)


def _layer(x, mem, layer, w_all, wmem_all, wb_all, wo_all, norm_gain, swa_q_gain, swa_k_gain,
           swa_sinks, conv_w, conv_b, lru_w_a, lru_b_a, lru_w_x, lru_b_x, lru_lambda,
           mem_norm_gain, mem_q_gain, mem_k_gain, consts):
    bdq, bdk, tri = consts
    row = lambda v: v.reshape(1, -1).astype(F32)
    mkt, mv = _memkv(mem, row(mem_norm_gain), wmem_all, row(mem_k_gain), layer)
    qg = row(jnp.tile(swa_q_gain, SWA_HEADS) * (SWA_HEAD_DIM ** -0.5))
    kg = row(jnp.tile(swa_k_gain, SWA_KV_HEADS))
    ya, yb, yc, ym = _branches(
        x, swa_sinks.reshape(1, SWA_HEADS).astype(F32), row(norm_gain), w_all, layer,
        qg, kg, bdq, bdk,
        conv_w.astype(F32), row(conv_b), _block_diag(lru_w_a).astype(BF16), row(lru_b_a),
        _block_diag(lru_w_x).astype(BF16), row(lru_b_x), row(lru_lambda),
        mkt, mv, row(mem_q_gain), tri)
    return _merge(x, row(norm_gain), w_all, ya, yb, yc, ym, wb_all, wo_all, layer)


def kernel(x, mem, norm_gain, w_in, swa_q_gain, swa_k_gain, swa_sinks, conv_w, conv_b, lru_w_a,
           lru_b_a, lru_w_x, lru_b_x, lru_lambda, mem_norm_gain, w_mem_kv, mem_q_gain,
           mem_k_gain, w_branch, w_out):
    depth = w_in.shape[0]
    consts = (
        _head_mean_matrix(SWA_HEADS * SWA_HEAD_DIM, SWA_HEAD_DIM),
        _head_mean_matrix(SWA_KV_HEADS * SWA_HEAD_DIM, SWA_HEAD_DIM),
        jnp.stack([jnp.tril(jnp.ones((TK_SB, TK_SB), F32), 0),
                   jnp.tril(jnp.ones((TK_SB, TK_SB), F32), -1)]).astype(BF16),
    )
    pad = jnp.zeros((depth, D_MODEL, 2 * OFF_MERGE - IN_WIDTH), w_in.dtype)
    w_all = jnp.concatenate([w_in[..., OFF_MERGE:], pad, w_in[..., :OFF_MERGE]],
                            axis=-1).astype(BF16)
    wmem_all, wb_all, wo_all = (w.astype(BF16) for w in (w_mem_kv, w_branch, w_out))
    for l in range(depth):
        x = _layer(x, mem, l, w_all, wmem_all, wb_all, wo_all, norm_gain[l], swa_q_gain[l],
                   swa_k_gain[l], swa_sinks[l], conv_w[l], conv_b[l], lru_w_a[l], lru_b_a[l],
                   lru_w_x[l], lru_b_x[l], lru_lambda[l], mem_norm_gain[l], mem_q_gain[l],
                   mem_k_gain[l], consts)
    return x
```

```python
import jax
import jax.numpy as jnp
from jax import lax
from jax.experimental import pallas as pl
from jax.experimental.pallas import tpu as pltpu

F32 = jnp.float32
BF16 = jnp.bfloat16

D_MODEL = 1024
BRANCH_WIDTH = 512
N_BRANCH = 4
SWA_HEAD_DIM = 64
SWA_HEADS = 8
SWA_KV_HEADS = 2
SWA_WINDOW = 128
SB_HEAD_DIM = 64
LRU_BLOCKS = 8
CONV_WIDTH = 4
LRU_C = 8.0
MEM_HEADS = 4
MEM_HEAD_DIM = 128
EPS = 1e-6

LANES = 128
SUBLANES = 8
HEAD_PAIRS = BRANCH_WIDTH // LANES

_SIZES = (512, 128, 128, 512, 512, 512, 512, 512, 512, 512, 512, 512, 4096)
_OFFS = tuple(sum(_SIZES[:i]) for i in range(len(_SIZES)))
(OFF_AQ, OFF_AK, OFF_AV, OFF_AG, OFF_BQ, OFF_BK, OFF_BV, OFF_BG,
 OFF_CX, OFF_CG, OFF_MQ, OFF_MG, OFF_MERGE) = _OFFS
IN_WIDTH = sum(_SIZES)

TM_BRANCH = 512
TM_MERGE = 1024
TK_SB = 256
LRU_CHUNK = 64
FILL_COLS = 256

VMEM_LIMIT = 62 * 1024 * 1024

SB_EXIT_CARRY = 88.0


def _dot(a, b):
    return jnp.dot(a, b, preferred_element_type=F32)


def _dot_nt(a, b):
    return lax.dot_general(a, b, (((1,), (1,)), ((), ())), preferred_element_type=F32)


def _sigmoid(x):
    return 1.0 / (1.0 + jnp.exp(-x))


def _silu(x):
    return x * _sigmoid(x)


def _rms_scale(x):
    return lax.rsqrt(jnp.mean(x * x, axis=-1, keepdims=True) + EPS)


def _swa_branch(sink_ref, q, g, kcat, vcat, qg, kg, bdq, bdk, at_start, o_ref):
    tq, sub = q.shape[0], SWA_WINDOW
    nkeys = tq + sub
    msq = _dot((q * q).astype(BF16), bdq)
    qn = (q * lax.rsqrt(msq + EPS) * qg).astype(BF16)
    msk = _dot((kcat * kcat).astype(BF16), bdk)
    kn_t = (kcat * lax.rsqrt(msk + EPS) * kg).T
    zeros = jnp.zeros((SWA_HEAD_DIM, nkeys), F32)

    def k_variant(half, kv):
        blk = kn_t[kv * SWA_HEAD_DIM:(kv + 1) * SWA_HEAD_DIM]
        parts = [blk, zeros] if half == 0 else [zeros, blk]
        return jnp.concatenate(parts, axis=0).astype(BF16)

    kt = [[k_variant(half, kv) for kv in range(SWA_KV_HEADS)] for half in range(2)]
    v_same = vcat.astype(BF16)
    v_swap = pltpu.roll(vcat, SWA_HEAD_DIM, axis=1).astype(BF16)

    r = lax.broadcasted_iota(jnp.int32, (sub, 2 * sub), 0)
    j = lax.broadcasted_iota(jnp.int32, (sub, 2 * sub), 1)
    diff = r - (j - sub)
    band = (diff >= 0) & (diff < SWA_WINDOW)
    first = jnp.where(at_start, sub, 0)
    band_first = band & (j >= first)
    lane = lax.broadcasted_iota(jnp.int32, (sub, LANES), 1)
    row2 = lax.broadcasted_iota(jnp.int32, (2 * sub, 1), 0)

    for sb in range(tq // sub):
        rows = slice(sb * sub, (sb + 1) * sub)
        keys = slice(sb * sub, (sb + 2) * sub)
        valid = band_first if sb == 0 else band
        valid2 = jnp.concatenate([valid, valid], axis=0)
        outs = [None] * SWA_HEADS
        for kv in range(SWA_KV_HEADS):
            for half in range(2):
                pa, pb = 2 * kv, 2 * kv + 1
                ha, hb = 2 * pa + half, 2 * pb + half
                q2 = jnp.concatenate([qn[rows, pa * LANES:(pa + 1) * LANES],
                                      qn[rows, pb * LANES:(pb + 1) * LANES]], axis=0)
                sc = _dot(q2, kt[half][kv][:, keys])
                sc = jnp.where(valid2, sc, -jnp.inf)
                sink = jnp.where(row2 < sub, sink_ref[0, ha], sink_ref[0, hb])
                m = jnp.maximum(jnp.max(sc, axis=-1, keepdims=True), sink)
                pe = jnp.exp(sc - m)
                den = jnp.sum(pe, axis=-1, keepdims=True) + jnp.exp(sink - m)
                vv = v_same if half == kv else v_swap
                res = _dot(pe.astype(BF16), vv[keys]) * (1.0 / den)
                outs[ha], outs[hb] = res[:sub], res[sub:]
        blocks = [jnp.where(lane < SWA_HEAD_DIM, outs[2 * p], outs[2 * p + 1])
                  for p in range(HEAD_PAIRS)]
        y = jnp.concatenate(blocks, axis=1)
        o_ref[0, rows, :] = (y * _silu(g[rows])).astype(BF16)


def _lru_branch(x, g, cw_ref, cb_ref, wa_ref, ba_ref, wx_ref, bx_ref, lam_ref,
                xe_ref, h_ref, hs_ref, fill, never):
    tc = x.shape[0]
    halo = SUBLANES
    xe_ref[halo:halo + tc, :] = x
    xc = cb_ref[...] + cw_ref[CONV_WIDTH - 1:CONV_WIDTH, :] * x
    for tap in range(CONV_WIDTH - 1):
        back = CONV_WIDTH - 1 - tap
        xc = xc + cw_ref[tap:tap + 1, :] * xe_ref[halo - back:halo - back + tc, :]
    xe_ref[0:halo, :] = x[tc - halo:, :]

    xb = xc.astype(BF16)
    rg = _sigmoid(_dot(xb, wa_ref[...]) + ba_ref[...])
    ig = _sigmoid(_dot(xb, wx_ref[...]) + bx_ref[...])
    lam = lam_ref[...]
    softplus_neg_lam = jnp.maximum(-lam, 0.0) + jnp.log1p(jnp.exp(-jnp.abs(lam)))
    rate = -LRU_C * softplus_neg_lam

    groups = LRU_CHUNK // SUBLANES
    sub = lax.broadcasted_iota(jnp.int32, (groups, SUBLANES, BRANCH_WIDTH), 1)
    h = h_ref[...]
    for start in range(0, tc, LRU_CHUNK):
        rows = slice(start, start + LRU_CHUNK)
        a = jnp.exp(rg[rows] * rate)
        bv = jnp.sqrt(1.0 - a * a) * (ig[rows] * xc[rows])
        a3 = a.reshape(groups, SUBLANES, BRANCH_WIDTH)
        b3 = bv.reshape(groups, SUBLANES, BRANCH_WIDTH)
        shift = 1
        while shift < SUBLANES:
            keep = sub >= shift
            a_sh = pltpu.roll(a3, shift, axis=1)
            b_sh = pltpu.roll(b3, shift, axis=1)
            b3 = jnp.where(keep, a3 * b_sh + b3, b3)
            a3 = jnp.where(keep, a3 * a_sh, a3)
            shift *= 2
        for gi in range(groups):
            hg = a3[gi] * h + b3[gi]
            hs_ref[start + gi * SUBLANES:start + (gi + 1) * SUBLANES, :] = hg
            h = hg[SUBLANES - 1:SUBLANES, :]
        piece_row = fill()
        if piece_row is not None:
            rate = jnp.where(never, piece_row, rate)
    h_ref[...] = h
    return hs_ref[...] * _silu(g)


def _mem_branch(q, g, kt_ref, v_ref, qg):
    scale = MEM_HEAD_DIM ** -0.5
    outs = []
    for h in range(MEM_HEADS):
        sl = slice(h * MEM_HEAD_DIM, (h + 1) * MEM_HEAD_DIM)
        qh = q[:, sl]
        qn = (qh * _rms_scale(qh) * qg).astype(BF16)
        sc = _dot(qn, kt_ref[0, sl, :]) * scale
        m = jnp.max(sc, axis=-1, keepdims=True)
        pe = jnp.exp(sc - m)
        den = jnp.sum(pe, axis=-1, keepdims=True)
        outs.append(_dot(pe.astype(BF16), v_ref[0, :, sl]) * (1.0 / den))
    return jnp.concatenate(outs, axis=1) * _silu(g)


def _memkv_kernel(mem_ref, g_ref, w_ref, kg_ref, kt_ref, v_ref):
    m = mem_ref[0]
    u = (m * _rms_scale(m) * g_ref[...]).astype(BF16)
    kv = _dot(u, w_ref[...])
    for h in range(MEM_HEADS):
        kh = kv[:, h * MEM_HEAD_DIM:(h + 1) * MEM_HEAD_DIM]
        kn = kh * _rms_scale(kh) * kg_ref[...]
        kt_ref[0, h * MEM_HEAD_DIM:(h + 1) * MEM_HEAD_DIM, :] = kn.T.astype(BF16)
    v_ref[0] = kv[:, BRANCH_WIDTH:].astype(BF16)


def _memkv(mem, gain, w_all, kg, layer):
    b, mlen, d = mem.shape
    return pl.pallas_call(
        _memkv_kernel,
        grid=(b,),
        in_specs=[
            pl.BlockSpec((1, mlen, d), lambda bi: (bi, 0, 0)),
            pl.BlockSpec((1, d), lambda bi: (0, 0)),
            pl.BlockSpec((None, d, 2 * BRANCH_WIDTH), lambda bi: (layer, 0, 0)),
            pl.BlockSpec((1, MEM_HEAD_DIM), lambda bi: (0, 0)),
        ],
        out_specs=[pl.BlockSpec((1, BRANCH_WIDTH, mlen), lambda bi: (bi, 0, 0)),
                   pl.BlockSpec((1, mlen, BRANCH_WIDTH), lambda bi: (bi, 0, 0))],
        out_shape=[jax.ShapeDtypeStruct((b, BRANCH_WIDTH, mlen), BF16),
                   jax.ShapeDtypeStruct((b, mlen, BRANCH_WIDTH), BF16)],
        compiler_params=pltpu.CompilerParams(
            dimension_semantics=("parallel",), vmem_limit_bytes=VMEM_LIMIT),
        name="memkv",
    )(mem, gain, w_all, kg)


def _sb_tile(q_half, kj, vj, tri_incl, carries, causal, causal_bf16):
    rows, tk = q_half[0].shape[0], kj.shape[0]
    outs, new_carries = [], []
    for a in range(2):
        z = _dot_nt(q_half[a], kj)
        zb = z.astype(BF16)
        sp = jnp.maximum(zb, 0) + jnp.log(1 + jnp.exp(-jnp.abs(zb)))
        if causal_bf16 is not None:
            sp = sp * causal_bf16
        cum = _dot(sp, tri_incl)
        carry = carries[a]
        arg = z - (cum + jnp.concatenate([carry] * (tk // LANES), axis=1))
        if causal is not None:
            arg = jnp.where(causal, arg, -jnp.inf)
        outs.append(_dot(jnp.exp(arg).astype(BF16), vj))
        new_carries.append(carry + jnp.broadcast_to(cum[:, 0:1], (rows, LANES)))
    return outs, new_carries


def _q_halves(q):
    lane = lax.broadcasted_iota(jnp.int32, q.shape, 1)
    zero = jnp.zeros_like(q)
    return [jnp.where(lane < SB_HEAD_DIM, q, zero), jnp.where(lane >= SB_HEAD_DIM, q, zero)]


def _branch_kernel(sink_ref, x_ref, g_ref, w_ref, qg_ref, kg_ref, bdq_ref, bdk_ref,
                   cw_ref, cb_ref, wa_ref, ba_ref, wx_ref, bx_ref, lam_ref,
                   mkt_ref, mv_ref, mqg_ref, tri_ref,
                   ya_ref, yb_ref, yc_ref, ym_ref,
                   kvp_ref, xe_ref, h_ref, hs_ref, gs_ref, qs_ref, kall_ref, vall_ref,
                   acc_ref, car_ref):
    i = pl.program_id(1)
    tm, tb = TM_BRANCH, TK_SB
    nsub = tm // tb
    never = i < 0

    @pl.when(i == 0)
    def _():
        kvp_ref[...] = jnp.zeros_like(kvp_ref)
        xe_ref[0:SUBLANES, :] = jnp.zeros((SUBLANES, BRANCH_WIDTH), F32)
        h_ref[...] = jnp.zeros_like(h_ref)

    x = x_ref[0]
    u = (x * _rms_scale(x) * g_ref[...]).astype(BF16)

    def proj(off, width):
        return _dot(u, w_ref[:, off:off + width])

    row0 = pl.multiple_of(i * tm, tm)

    def b_piece(kind, col):
        def emit():
            blocks = range(col // LANES, (col + FILL_COLS) // LANES)
            if kind == "g":
                res = proj(OFF_BG + col, FILL_COLS)
                gs_ref[:, col:col + FILL_COLS] = _silu(res)
            elif kind == "q":
                res = proj(OFF_BQ + col, FILL_COLS) * (SB_HEAD_DIM ** -0.5)
                val = res.astype(BF16)
                for n, p in enumerate(blocks):
                    qs_ref[p] = val[:, n * LANES:(n + 1) * LANES]
            else:
                off, ref = (OFF_BK, kall_ref) if kind == "k" else (OFF_BV, vall_ref)
                res = proj(off + col, FILL_COLS)
                val = res.astype(BF16)
                for n, p in enumerate(blocks):
                    ref[p, pl.ds(row0, tm), :] = val[:, n * LANES:(n + 1) * LANES]
            return res[0:1, :]
        return emit

    pending = iter([b_piece(kind, col) for kind in "qkvg"
                    for col in range(0, BRANCH_WIDTH, FILL_COLS)])

    def fill():
        emit = next(pending, None)
        if emit is None:
            return None
        row = emit()
        return jnp.concatenate([row] * (BRANCH_WIDTH // FILL_COLS), axis=1)

    yc = _lru_branch(proj(OFF_CX, 512), proj(OFF_CG, 512), cw_ref, cb_ref, wa_ref, ba_ref,
                     wx_ref, bx_ref, lam_ref, xe_ref, h_ref, hs_ref, fill, never)
    yc_ref[0] = yc.astype(BF16)
    for emit in pending:
        emit()

    tri = tri_ref[0]
    strict = tri_ref[1]
    r = lax.broadcasted_iota(jnp.int32, (tb, tb), 0)
    c = lax.broadcasted_iota(jnp.int32, (tb, tb), 1)
    causal = c < r
    first = i * nsub

    def history_tile(p, jt):
        start = pl.multiple_of(jt * tb, tb)
        return kall_ref[p, pl.ds(start, tb), :], vall_ref[p, pl.ds(start, tb), :]

    def sb_pair(p):
        k0, v0 = history_tile(p, jnp.maximum(first - 1, 0))
        tiles = [(k0, jnp.where(first > 0, v0, jnp.zeros_like(v0)))]
        tiles += [history_tile(p, first + sb) for sb in range(nsub)]
        zeros = [jnp.zeros((tb, LANES), F32)] * 2
        for sb in range(nsub):
            qh = _q_halves(qs_ref[p, sb * tb:(sb + 1) * tb, :])
            out_d, car_d = _sb_tile(qh, *tiles[sb + 1], tri, zeros, causal, strict)
            out_l, car_l = _sb_tile(qh, *tiles[sb], tri, car_d, None, None)
            for a in range(2):
                acc_ref[p, sb, a] = out_d[a] + out_l[a]
                car_ref[p, sb, a] = car_l[a]

    for p in range(HEAD_PAIRS):
        sb_pair(p)

    kva = proj(OFF_AK, 2 * LANES)
    kvp = kvp_ref[...]
    kcat = jnp.concatenate([kvp[:, :LANES], kva[:, :LANES]], axis=0)
    vcat = jnp.concatenate([kvp[:, LANES:], kva[:, LANES:]], axis=0)
    kvp_ref[...] = kva[tm - SWA_WINDOW:, :]
    _swa_branch(sink_ref, proj(OFF_AQ, 512), proj(OFF_AG, 512), kcat, vcat, qg_ref[...],
                kg_ref[...], bdq_ref[...], bdk_ref[...], i == 0, ya_ref)

    ym = _mem_branch(proj(OFF_MQ, 512), proj(OFF_MG, 512), mkt_ref, mv_ref, mqg_ref[...])
    ym_ref[0] = ym.astype(BF16)

    def continue_left(p, sb):
        qh = _q_halves(qs_ref[p, pl.ds(pl.multiple_of(sb * tb, tb), tb), :])

        def not_done():
            return (jnp.min(car_ref[p, sb]) < SB_EXIT_CARRY).astype(jnp.int32)

        def cond(state):
            jt, live = state
            return jnp.logical_and(jt >= 0, live > 0)

        def body(state):
            jt, _ = state
            kj, vj = history_tile(p, jt)
            outs, cars = _sb_tile(qh, kj, vj, tri, [car_ref[p, sb, 0], car_ref[p, sb, 1]],
                                  None, None)
            for a in range(2):
                acc_ref[p, sb, a] += outs[a]
                car_ref[p, sb, a] = cars[a]
            return jt - 1, not_done()

        lax.while_loop(cond, body, (first + sb - 2, not_done()))

    @pl.when(jnp.min(car_ref[...]) < SB_EXIT_CARRY)
    def _():
        def over_pairs(p, unused):
            def over_subs(sb, unused2):
                continue_left(p, sb)
                return unused2
            return lax.fori_loop(0, nsub, over_subs, unused)
        lax.fori_loop(0, HEAD_PAIRS, over_pairs, 0)

    lane = lax.broadcasted_iota(jnp.int32, (tb, LANES), 1)
    for p in range(HEAD_PAIRS):
        for sb in range(nsub):
            rows = slice(sb * tb, (sb + 1) * tb)
            y = jnp.where(lane < SB_HEAD_DIM, acc_ref[p, sb, 0], acc_ref[p, sb, 1])
            yb_ref[0, rows, p * LANES:(p + 1) * LANES] = (
                y * gs_ref[rows, p * LANES:(p + 1) * LANES]).astype(BF16)


def _branches(x, sinks, gain, w_all, layer, qg, kg, bdq, bdk, cw, cb, wa, ba, wx, bx, lam, mkt,
              mv, mqg, tri):
    b, s, d = x.shape
    w = BRANCH_WIDTH
    mlen = mv.shape[1]
    tm, tb = TM_BRANCH, TK_SB
    const = lambda *shape: pl.BlockSpec(shape, lambda bi, i: (0,) * len(shape),
                                        pipeline_mode=pl.Buffered(1))
    row = pl.BlockSpec((1, tm, w), lambda bi, i: (bi, i, 0))
    out = jax.ShapeDtypeStruct((b, s, w), BF16)
    return pl.pallas_call(
        _branch_kernel,
        grid=(b, s // tm),
        in_specs=[
            pl.BlockSpec(memory_space=pltpu.SMEM),
            pl.BlockSpec((1, tm, d), lambda bi, i: (bi, i, 0)),
            const(1, d),
            pl.BlockSpec((None, d, OFF_MERGE), lambda bi, i: (layer, 0, 0),
                         pipeline_mode=pl.Buffered(1)),
            const(1, w), const(1, LANES), const(w, w), const(LANES, LANES),
            const(CONV_WIDTH, w), const(1, w), const(w, w), const(1, w), const(w, w),
            const(1, w), const(1, w),
            pl.BlockSpec((1, w, mlen), lambda bi, i: (bi, 0, 0)),
            pl.BlockSpec((1, mlen, w), lambda bi, i: (bi, 0, 0)),
            const(1, MEM_HEAD_DIM),
            const(2, tb, tb),
        ],
        out_specs=[row, row, row, row],
        out_shape=[out, out, out, out],
        scratch_shapes=[
            pltpu.VMEM((SWA_WINDOW, 2 * LANES), F32),
            pltpu.VMEM((tm + SUBLANES, w), F32),
            pltpu.VMEM((1, w), F32),
            pltpu.VMEM((tm, w), F32),
            pltpu.VMEM((tm, w), F32),
            pltpu.VMEM((HEAD_PAIRS, tm, LANES), BF16),
            pltpu.VMEM((HEAD_PAIRS, s, LANES), BF16),
            pltpu.VMEM((HEAD_PAIRS, s, LANES), BF16),
            pltpu.VMEM((HEAD_PAIRS, tm // tb, 2, tb, LANES), F32),
            pltpu.VMEM((HEAD_PAIRS, tm // tb, 2, tb, LANES), F32),
        ],
        compiler_params=pltpu.CompilerParams(
            dimension_semantics=("parallel", "arbitrary"), vmem_limit_bytes=VMEM_LIMIT),
        name="branches",
    )(sinks, x, gain, w_all, qg, kg, bdq, bdk, cw, cb, wa, ba, wx, bx, lam, mkt, mv, mqg, tri)


def _merge_kernel(x_ref, g_ref, w_ref, ya_ref, yb_ref, yc_ref, ym_ref, wb_ref, wo_ref, o_ref):
    x = x_ref[0]
    u = (x * _rms_scale(x) * g_ref[...]).astype(BF16)
    mixed = None
    for n, y_ref in enumerate((ya_ref, yb_ref, yc_ref, ym_ref)):
        up = _dot(y_ref[0], wb_ref[n])
        col = OFF_MERGE + n * D_MODEL
        gate = _sigmoid(_dot(u, w_ref[:, col:col + D_MODEL]))
        mixed = gate * up if mixed is None else mixed + gate * up
    o_ref[0] = x + _dot(mixed.astype(BF16), wo_ref[...])


def _merge(x, gain, w_all, ya, yb, yc, ym, wb_all, wo_all, layer):
    b, s, d = x.shape
    tm = TM_MERGE
    br = pl.BlockSpec((1, tm, BRANCH_WIDTH), lambda bi, i: (bi, i, 0))
    return pl.pallas_call(
        _merge_kernel,
        grid=(b, s // tm),
        in_specs=[
            pl.BlockSpec((1, tm, d), lambda bi, i: (bi, i, 0)),
            pl.BlockSpec((1, d), lambda bi, i: (0, 0)),
            pl.BlockSpec((None, d, IN_WIDTH), lambda bi, i: (layer, 0, 0),
                         pipeline_mode=pl.Buffered(1)),
            br, br, br, br,
            pl.BlockSpec((None, N_BRANCH, BRANCH_WIDTH, d), lambda bi, i: (layer, 0, 0, 0),
                         pipeline_mode=pl.Buffered(1)),
            pl.BlockSpec((None, d, d), lambda bi, i: (layer, 0, 0),
                         pipeline_mode=pl.Buffered(1)),
        ],
        out_specs=pl.BlockSpec((1, tm, d), lambda bi, i: (bi, i, 0)),
        out_shape=jax.ShapeDtypeStruct((b, s, d), F32),
        compiler_params=pltpu.CompilerParams(
            dimension_semantics=("parallel", "parallel"), vmem_limit_bytes=VMEM_LIMIT),
        name="merge_out",
    )(x, gain, w_all, ya, yb, yc, ym, wb_all, wo_all)


def _head_mean_matrix(width, head_dim):
    idx = jnp.arange(width) // head_dim
    return jnp.where(idx[:, None] == idx[None, :], 1.0 / head_dim, 0.0).astype(BF16)


def _block_diag(w):
    n, c, d = w.shape
    eye = jnp.eye(n, dtype=w.dtype)
    return jnp.einsum('ncd,nm->ncmd', w, eye).reshape(n * c, n * d)


def _layer(x, mem, layer, w_all, wmem_all, wb_all, wo_all, norm_gain, swa_q_gain, swa_k_gain,
           swa_sinks, conv_w, conv_b, lru_w_a, lru_b_a, lru_w_x, lru_b_x, lru_lambda,
           mem_norm_gain, mem_q_gain, mem_k_gain, consts):
    bdq, bdk, tri = consts
    row = lambda v: v.reshape(1, -1).astype(F32)
    mkt, mv = _memkv(mem, row(mem_norm_gain), wmem_all, row(mem_k_gain), layer)
    qg = row(jnp.tile(swa_q_gain, SWA_HEADS) * (SWA_HEAD_DIM ** -0.5))
    kg = row(jnp.tile(swa_k_gain, SWA_KV_HEADS))
    ya, yb, yc, ym = _branches(
        x, swa_sinks.reshape(1, SWA_HEADS).astype(F32), row(norm_gain), w_all, layer,
        qg, kg, bdq, bdk,
        conv_w.astype(F32), row(conv_b), _block_diag(lru_w_a).astype(BF16), row(lru_b_a),
        _block_diag(lru_w_x).astype(BF16), row(lru_b_x), row(lru_lambda),
        mkt, mv, row(mem_q_gain), tri)
    return _merge(x, row(norm_gain), w_all, ya, yb, yc, ym, wb_all, wo_all, layer)


def kernel(x, mem, norm_gain, w_in, swa_q_gain, swa_k_gain, swa_sinks, conv_w, conv_b, lru_w_a,
           lru_b_a, lru_w_x, lru_b_x, lru_lambda, mem_norm_gain, w_mem_kv, mem_q_gain,
           mem_k_gain, w_branch, w_out):
    depth = w_in.shape[0]
    consts = (
        _head_mean_matrix(SWA_HEADS * SWA_HEAD_DIM, SWA_HEAD_DIM),
        _head_mean_matrix(SWA_KV_HEADS * SWA_HEAD_DIM, SWA_HEAD_DIM),
        jnp.stack([jnp.tril(jnp.ones((TK_SB, TK_SB), F32), 0),
                   jnp.tril(jnp.ones((TK_SB, TK_SB), F32), -1)]).astype(BF16),
    )
    w_all, wmem_all, wb_all, wo_all = (w.astype(BF16) for w in (w_in, w_mem_kv, w_branch, w_out))
    for l in range(depth):
        x = _layer(x, mem, l, w_all, wmem_all, wb_all, wo_all, norm_gain[l], swa_q_gain[l],
                   swa_k_gain[l], swa_sinks[l], conv_w[l], conv_b[l], lru_w_a[l], lru_b_a[l],
                   lru_w_x[l], lru_b_x[l], lru_lambda[l], mem_norm_gain[l], mem_q_gain[l],
                   mem_k_gain[l], consts)
    return x
```

```python
import jax
import jax.numpy as jnp
from jax import lax
from jax.experimental import pallas as pl
from jax.experimental.pallas import tpu as pltpu

F32 = jnp.float32
BF16 = jnp.bfloat16

D_MODEL = 1024
BRANCH_WIDTH = 512
N_BRANCH = 4
SWA_HEAD_DIM = 64
SWA_HEADS = 8
SWA_KV_HEADS = 2
SWA_WINDOW = 128
SB_HEAD_DIM = 64
LRU_BLOCKS = 8
CONV_WIDTH = 4
LRU_C = 8.0
MEM_HEADS = 4
MEM_HEAD_DIM = 128
EPS = 1e-6

LANES = 128
SUBLANES = 8
MXU_TILE = 256
HEAD_PAIRS = BRANCH_WIDTH // LANES

_SIZES = (512, 128, 128, 512, 512, 512, 512, 512, 512, 512, 512, 512, 4096)
_OFFS = tuple(sum(_SIZES[:i]) for i in range(len(_SIZES)))
(OFF_AQ, OFF_AK, OFF_AV, OFF_AG, OFF_BQ, OFF_BK, OFF_BV, OFF_BG,
 OFF_CX, OFF_CG, OFF_MQ, OFF_MG, OFF_MERGE) = _OFFS
IN_WIDTH = sum(_SIZES)

TM_BRANCH = 512
TM_MERGE = 1024
TK_SB = 256
LRU_CHUNK = 128
FILL_COLS = 512

VMEM_LIMIT = 62 * 1024 * 1024

SB_EXIT_CARRY = 88.0


def _dot(a, b):
    return jnp.dot(a, b, preferred_element_type=F32)


def _dot_nt(a, b):
    return lax.dot_general(a, b, (((1,), (1,)), ((), ())), preferred_element_type=F32)


def _sigmoid(x):
    return 1.0 / (1.0 + jnp.exp(-x))


def _silu(x):
    return x * _sigmoid(x)


def _rms_scale(x):
    return lax.rsqrt(jnp.mean(x * x, axis=-1, keepdims=True) + EPS)


def _swa_branch(sink_ref, q, g, kcat, vcat, qg, kg, bdq, bdk, at_start, o_ref):
    tq, sub = q.shape[0], SWA_WINDOW
    nkeys = tq + sub
    q2 = (q * q).astype(BF16)
    msq = jnp.concatenate([_dot(q2[:, t * MXU_TILE:(t + 1) * MXU_TILE], bdq)
                           for t in range(q.shape[1] // MXU_TILE)], axis=1)
    qn = (q * lax.rsqrt(msq + EPS) * qg).astype(BF16)
    msk = _dot((kcat * kcat).astype(BF16), bdk)
    kn_t = (kcat * lax.rsqrt(msk + EPS) * kg).T
    zeros = jnp.zeros((SWA_HEAD_DIM, nkeys), F32)

    def k_variant(half, kv):
        blk = kn_t[kv * SWA_HEAD_DIM:(kv + 1) * SWA_HEAD_DIM]
        parts = [blk, zeros] if half == 0 else [zeros, blk]
        return jnp.concatenate(parts, axis=0).astype(BF16)

    kt = [[k_variant(half, kv) for kv in range(SWA_KV_HEADS)] for half in range(2)]
    v_same = vcat.astype(BF16)
    v_swap = pltpu.roll(vcat, SWA_HEAD_DIM, axis=1).astype(BF16)

    r = lax.broadcasted_iota(jnp.int32, (sub, 2 * sub), 0)
    j = lax.broadcasted_iota(jnp.int32, (sub, 2 * sub), 1)
    diff = r - (j - sub)
    band = (diff >= 0) & (diff < SWA_WINDOW)
    first = jnp.where(at_start, sub, 0)
    band_first = band & (j >= first)
    lane = lax.broadcasted_iota(jnp.int32, (sub, LANES), 1)
    row2 = lax.broadcasted_iota(jnp.int32, (2 * sub, 1), 0)

    for sb in range(tq // sub):
        rows = slice(sb * sub, (sb + 1) * sub)
        keys = slice(sb * sub, (sb + 2) * sub)
        valid = band_first if sb == 0 else band
        valid2 = jnp.concatenate([valid, valid], axis=0)
        outs = [None] * SWA_HEADS
        for kv in range(SWA_KV_HEADS):
            for half in range(2):
                pa, pb = 2 * kv, 2 * kv + 1
                ha, hb = 2 * pa + half, 2 * pb + half
                q2 = jnp.concatenate([qn[rows, pa * LANES:(pa + 1) * LANES],
                                      qn[rows, pb * LANES:(pb + 1) * LANES]], axis=0)
                sc = _dot(q2, kt[half][kv][:, keys])
                sc = jnp.where(valid2, sc, -jnp.inf)
                sink = jnp.where(row2 < sub, sink_ref[0, ha], sink_ref[0, hb])
                m = jnp.maximum(jnp.max(sc, axis=-1, keepdims=True), sink)
                pe = jnp.exp(sc - m)
                den = jnp.sum(pe, axis=-1, keepdims=True) + jnp.exp(sink - m)
                vv = v_same if half == kv else v_swap
                res = _dot(pe.astype(BF16), vv[keys]) * (1.0 / den)
                outs[ha], outs[hb] = res[:sub], res[sub:]
        blocks = [jnp.where(lane < SWA_HEAD_DIM, outs[2 * p], outs[2 * p + 1])
                  for p in range(HEAD_PAIRS)]
        y = jnp.concatenate(blocks, axis=1)
        o_ref[0, rows, :] = (y * _silu(g[rows])).astype(BF16)


def _lru_branch(x, g, cw_ref, cb_ref, wa_ref, ba_ref, wx_ref, bx_ref, lam_ref,
                xe_ref, h_ref, hs_ref, fill, never):
    tc = x.shape[0]
    halo = SUBLANES
    xe_ref[halo:halo + tc, :] = x
    xc = cb_ref[...] + cw_ref[CONV_WIDTH - 1:CONV_WIDTH, :] * x
    for tap in range(CONV_WIDTH - 1):
        back = CONV_WIDTH - 1 - tap
        xc = xc + cw_ref[tap:tap + 1, :] * xe_ref[halo - back:halo - back + tc, :]
    xe_ref[0:halo, :] = x[tc - halo:, :]

    xb = xc.astype(BF16)

    def gate(w_ref, b_ref):
        pre = [_dot(xb[:, t * MXU_TILE:(t + 1) * MXU_TILE], w_ref[t])
               for t in range(BRANCH_WIDTH // MXU_TILE)]
        return _sigmoid(jnp.concatenate(pre, axis=1) + b_ref[...])

    rg = gate(wa_ref, ba_ref)
    ig = gate(wx_ref, bx_ref)
    lam = lam_ref[...]
    softplus_neg_lam = jnp.maximum(-lam, 0.0) + jnp.log1p(jnp.exp(-jnp.abs(lam)))
    rate = -LRU_C * softplus_neg_lam

    groups = LRU_CHUNK // SUBLANES
    sub = lax.broadcasted_iota(jnp.int32, (groups, SUBLANES, BRANCH_WIDTH), 1)
    h = h_ref[...]
    for start in range(0, tc, LRU_CHUNK):
        rows = slice(start, start + LRU_CHUNK)
        a = jnp.exp(rg[rows] * rate)
        bv = jnp.sqrt(1.0 - a * a) * (ig[rows] * xc[rows])
        a3 = a.reshape(groups, SUBLANES, BRANCH_WIDTH)
        b3 = bv.reshape(groups, SUBLANES, BRANCH_WIDTH)
        shift = 1
        while shift < SUBLANES:
            keep = sub >= shift
            a_sh = pltpu.roll(a3, shift, axis=1)
            b_sh = pltpu.roll(b3, shift, axis=1)
            b3 = jnp.where(keep, a3 * b_sh + b3, b3)
            a3 = jnp.where(keep, a3 * a_sh, a3)
            shift *= 2
        for gi in range(groups):
            hg = a3[gi] * h + b3[gi]
            hs_ref[start + gi * SUBLANES:start + (gi + 1) * SUBLANES, :] = hg
            h = hg[SUBLANES - 1:SUBLANES, :]
        piece_row = fill()
        if piece_row is not None:
            rate = jnp.where(never, piece_row, rate)
    h_ref[...] = h
    return hs_ref[...] * _silu(g)


def _mem_branch(q, g, kt_ref, v_ref, qg):
    scale = MEM_HEAD_DIM ** -0.5
    outs = []
    for h in range(MEM_HEADS):
        sl = slice(h * MEM_HEAD_DIM, (h + 1) * MEM_HEAD_DIM)
        qh = q[:, sl]
        qn = (qh * _rms_scale(qh) * qg).astype(BF16)
        sc = _dot(qn, kt_ref[0, sl, :]) * scale
        m = jnp.max(sc, axis=-1, keepdims=True)
        pe = jnp.exp(sc - m)
        den = jnp.sum(pe, axis=-1, keepdims=True)
        outs.append(_dot(pe.astype(BF16), v_ref[0, :, sl]) * (1.0 / den))
    return jnp.concatenate(outs, axis=1) * _silu(g)


def _memkv_kernel(mem_ref, g_ref, w_ref, kg_ref, kt_ref, v_ref):
    m = mem_ref[0]
    u = (m * _rms_scale(m) * g_ref[...]).astype(BF16)
    kv = _dot(u, w_ref[...])
    for h in range(MEM_HEADS):
        kh = kv[:, h * MEM_HEAD_DIM:(h + 1) * MEM_HEAD_DIM]
        kn = kh * _rms_scale(kh) * kg_ref[...]
        kt_ref[0, h * MEM_HEAD_DIM:(h + 1) * MEM_HEAD_DIM, :] = kn.T.astype(BF16)
    v_ref[0] = kv[:, BRANCH_WIDTH:].astype(BF16)


def _memkv(mem, gain, w_all, kg, layer):
    b, mlen, d = mem.shape
    return pl.pallas_call(
        _memkv_kernel,
        grid=(b,),
        in_specs=[
            pl.BlockSpec((1, mlen, d), lambda bi: (bi, 0, 0)),
            pl.BlockSpec((1, d), lambda bi: (0, 0)),
            pl.BlockSpec((None, d, 2 * BRANCH_WIDTH), lambda bi: (layer, 0, 0)),
            pl.BlockSpec((1, MEM_HEAD_DIM), lambda bi: (0, 0)),
        ],
        out_specs=[pl.BlockSpec((1, BRANCH_WIDTH, mlen), lambda bi: (bi, 0, 0)),
                   pl.BlockSpec((1, mlen, BRANCH_WIDTH), lambda bi: (bi, 0, 0))],
        out_shape=[jax.ShapeDtypeStruct((b, BRANCH_WIDTH, mlen), BF16),
                   jax.ShapeDtypeStruct((b, mlen, BRANCH_WIDTH), BF16)],
        compiler_params=pltpu.CompilerParams(
            dimension_semantics=("parallel",), vmem_limit_bytes=VMEM_LIMIT),
        name="memkv",
    )(mem, gain, w_all, kg)


def _sb_tile(q_half, kj, vj, tri_incl, carries, causal, causal_bf16):
    rows, tk = q_half[0].shape[0], kj.shape[0]
    outs, new_carries = [], []
    for a in range(2):
        z = _dot_nt(q_half[a], kj)
        zb = z.astype(BF16)
        sp = jnp.maximum(zb, 0) + jnp.log(1 + jnp.exp(-jnp.abs(zb)))
        if causal_bf16 is not None:
            sp = sp * causal_bf16
        cum = _dot(sp, tri_incl)
        carry = carries[a]
        arg = z - (cum + jnp.concatenate([carry] * (tk // LANES), axis=1))
        if causal is not None:
            arg = jnp.where(causal, arg, -jnp.inf)
        outs.append(_dot(jnp.exp(arg).astype(BF16), vj))
        new_carries.append(carry + jnp.broadcast_to(cum[:, 0:1], (rows, LANES)))
    return outs, new_carries


def _q_halves(q):
    lane = lax.broadcasted_iota(jnp.int32, q.shape, 1)
    zero = jnp.zeros_like(q)
    return [jnp.where(lane < SB_HEAD_DIM, q, zero), jnp.where(lane >= SB_HEAD_DIM, q, zero)]


def _branch_kernel(sink_ref, x_ref, g_ref, w_ref, qg_ref, kg_ref, bdq_ref, bdk_ref,
                   cw_ref, cb_ref, wa_ref, ba_ref, wx_ref, bx_ref, lam_ref,
                   mkt_ref, mv_ref, mqg_ref, tri_ref,
                   ya_ref, yb_ref, yc_ref, ym_ref,
                   kvp_ref, xe_ref, h_ref, hs_ref, gs_ref, qs_ref, kall_ref, vall_ref,
                   acc_ref, car_ref):
    i = pl.program_id(1)
    tm, tb = TM_BRANCH, TK_SB
    nsub = tm // tb
    never = i < 0

    @pl.when(i == 0)
    def _():
        kvp_ref[...] = jnp.zeros_like(kvp_ref)
        xe_ref[0:SUBLANES, :] = jnp.zeros((SUBLANES, BRANCH_WIDTH), F32)
        h_ref[...] = jnp.zeros_like(h_ref)

    x = x_ref[0]
    u = (x * _rms_scale(x) * g_ref[...]).astype(BF16)

    def proj(off, width):
        return _dot(u, w_ref[:, off:off + width])

    row0 = pl.multiple_of(i * tm, tm)

    def b_piece(kind, col):
        def emit():
            blocks = range(col // LANES, (col + FILL_COLS) // LANES)
            if kind == "g":
                res = proj(OFF_BG + col, FILL_COLS)
                gs_ref[:, col:col + FILL_COLS] = _silu(res)
            elif kind == "q":
                res = proj(OFF_BQ + col, FILL_COLS) * (SB_HEAD_DIM ** -0.5)
                val = res.astype(BF16)
                for n, p in enumerate(blocks):
                    qs_ref[p] = val[:, n * LANES:(n + 1) * LANES]
            else:
                off, ref = (OFF_BK, kall_ref) if kind == "k" else (OFF_BV, vall_ref)
                res = proj(off + col, FILL_COLS)
                val = res.astype(BF16)
                for n, p in enumerate(blocks):
                    ref[p, pl.ds(row0, tm), :] = val[:, n * LANES:(n + 1) * LANES]
            return res[0:1, :]
        return emit

    pending = iter([b_piece(kind, col) for kind in "qkvg"
                    for col in range(0, BRANCH_WIDTH, FILL_COLS)])

    def fill():
        emit = next(pending, None)
        if emit is None:
            return None
        row = emit()
        return jnp.concatenate([row] * (BRANCH_WIDTH // FILL_COLS), axis=1)

    yc = _lru_branch(proj(OFF_CX, 512), proj(OFF_CG, 512), cw_ref, cb_ref, wa_ref, ba_ref,
                     wx_ref, bx_ref, lam_ref, xe_ref, h_ref, hs_ref, fill, never)
    yc_ref[0] = yc.astype(BF16)
    for emit in pending:
        emit()

    tri = tri_ref[0]
    strict = tri_ref[1]
    r = lax.broadcasted_iota(jnp.int32, (tb, tb), 0)
    c = lax.broadcasted_iota(jnp.int32, (tb, tb), 1)
    causal = c < r
    first = i * nsub

    def history_tile(p, jt):
        start = pl.multiple_of(jt * tb, tb)
        return kall_ref[p, pl.ds(start, tb), :], vall_ref[p, pl.ds(start, tb), :]

    def sb_pair(p):
        k0, v0 = history_tile(p, jnp.maximum(first - 1, 0))
        tiles = [(k0, jnp.where(first > 0, v0, jnp.zeros_like(v0)))]
        tiles += [history_tile(p, first + sb) for sb in range(nsub)]
        zeros = [jnp.zeros((tb, LANES), F32)] * 2
        for sb in range(nsub):
            qh = _q_halves(qs_ref[p, sb * tb:(sb + 1) * tb, :])
            out_d, car_d = _sb_tile(qh, *tiles[sb + 1], tri, zeros, causal, strict)
            out_l, car_l = _sb_tile(qh, *tiles[sb], tri, car_d, None, None)
            for a in range(2):
                acc_ref[p, sb, a] = out_d[a] + out_l[a]
                car_ref[p, sb, a] = car_l[a]

    for p in range(HEAD_PAIRS):
        sb_pair(p)

    kva = proj(OFF_AK, 2 * LANES)
    kvp = kvp_ref[...]
    kcat = jnp.concatenate([kvp[:, :LANES], kva[:, :LANES]], axis=0)
    vcat = jnp.concatenate([kvp[:, LANES:], kva[:, LANES:]], axis=0)
    kvp_ref[...] = kva[tm - SWA_WINDOW:, :]
    _swa_branch(sink_ref, proj(OFF_AQ, 512), proj(OFF_AG, 512), kcat, vcat, qg_ref[...],
                kg_ref[...], bdq_ref[...], bdk_ref[...], i == 0, ya_ref)

    ym = _mem_branch(proj(OFF_MQ, 512), proj(OFF_MG, 512), mkt_ref, mv_ref, mqg_ref[...])
    ym_ref[0] = ym.astype(BF16)

    def continue_left(p, sb):
        qh = _q_halves(qs_ref[p, pl.ds(pl.multiple_of(sb * tb, tb), tb), :])

        def not_done():
            return (jnp.min(car_ref[p, sb]) < SB_EXIT_CARRY).astype(jnp.int32)

        def cond(state):
            jt, live = state
            return jnp.logical_and(jt >= 0, live > 0)

        def body(state):
            jt, _ = state
            kj, vj = history_tile(p, jt)
            outs, cars = _sb_tile(qh, kj, vj, tri, [car_ref[p, sb, 0], car_ref[p, sb, 1]],
                                  None, None)
            for a in range(2):
                acc_ref[p, sb, a] += outs[a]
                car_ref[p, sb, a] = cars[a]
            return jt - 1, not_done()

        lax.while_loop(cond, body, (first + sb - 2, not_done()))

    @pl.when(jnp.min(car_ref[...]) < SB_EXIT_CARRY)
    def _():
        def over_pairs(p, unused):
            def over_subs(sb, unused2):
                continue_left(p, sb)
                return unused2
            return lax.fori_loop(0, nsub, over_subs, unused)
        lax.fori_loop(0, HEAD_PAIRS, over_pairs, 0)

    lane = lax.broadcasted_iota(jnp.int32, (tb, LANES), 1)
    for p in range(HEAD_PAIRS):
        for sb in range(nsub):
            rows = slice(sb * tb, (sb + 1) * tb)
            y = jnp.where(lane < SB_HEAD_DIM, acc_ref[p, sb, 0], acc_ref[p, sb, 1])
            yb_ref[0, rows, p * LANES:(p + 1) * LANES] = (
                y * gs_ref[rows, p * LANES:(p + 1) * LANES]).astype(BF16)


def _branches(x, sinks, gain, w_all, layer, qg, kg, bdq, bdk, cw, cb, wa, ba, wx, bx, lam, mkt,
              mv, mqg, tri):
    b, s, d = x.shape
    w = BRANCH_WIDTH
    mlen = mv.shape[1]
    tm, tb = TM_BRANCH, TK_SB
    const = lambda *shape: pl.BlockSpec(shape, lambda bi, i: (0,) * len(shape),
                                        pipeline_mode=pl.Buffered(1))
    row = pl.BlockSpec((1, tm, w), lambda bi, i: (bi, i, 0))
    out = jax.ShapeDtypeStruct((b, s, w), BF16)
    return pl.pallas_call(
        _branch_kernel,
        grid=(b, s // tm),
        in_specs=[
            pl.BlockSpec(memory_space=pltpu.SMEM),
            pl.BlockSpec((1, tm, d), lambda bi, i: (bi, i, 0)),
            const(1, d),
            pl.BlockSpec((None, d, OFF_MERGE), lambda bi, i: (layer, 0, 0),
                         pipeline_mode=pl.Buffered(1)),
            const(1, w), const(1, LANES), const(MXU_TILE, MXU_TILE), const(LANES, LANES),
            const(CONV_WIDTH, w), const(1, w), const(w // MXU_TILE, MXU_TILE, MXU_TILE),
            const(1, w), const(w // MXU_TILE, MXU_TILE, MXU_TILE),
            const(1, w), const(1, w),
            pl.BlockSpec((1, w, mlen), lambda bi, i: (bi, 0, 0)),
            pl.BlockSpec((1, mlen, w), lambda bi, i: (bi, 0, 0)),
            const(1, MEM_HEAD_DIM),
            const(2, tb, tb),
        ],
        out_specs=[row, row, row, row],
        out_shape=[out, out, out, out],
        scratch_shapes=[
            pltpu.VMEM((SWA_WINDOW, 2 * LANES), F32),
            pltpu.VMEM((tm + SUBLANES, w), F32),
            pltpu.VMEM((1, w), F32),
            pltpu.VMEM((tm, w), F32),
            pltpu.VMEM((tm, w), F32),
            pltpu.VMEM((HEAD_PAIRS, tm, LANES), BF16),
            pltpu.VMEM((HEAD_PAIRS, s, LANES), BF16),
            pltpu.VMEM((HEAD_PAIRS, s, LANES), BF16),
            pltpu.VMEM((HEAD_PAIRS, tm // tb, 2, tb, LANES), F32),
            pltpu.VMEM((HEAD_PAIRS, tm // tb, 2, tb, LANES), F32),
        ],
        compiler_params=pltpu.CompilerParams(
            dimension_semantics=("parallel", "arbitrary"), vmem_limit_bytes=VMEM_LIMIT),
        name="branches",
    )(sinks, x, gain, w_all, qg, kg, bdq, bdk, cw, cb, wa, ba, wx, bx, lam, mkt, mv, mqg, tri)


def _merge_kernel(x_ref, g_ref, w_ref, ya_ref, yb_ref, yc_ref, ym_ref, wb_ref, wo_ref, o_ref):
    x = x_ref[0]
    u = (x * _rms_scale(x) * g_ref[...]).astype(BF16)
    mixed = None
    for n, y_ref in enumerate((ya_ref, yb_ref, yc_ref, ym_ref)):
        up = _dot(y_ref[0], wb_ref[n])
        col = OFF_MERGE + n * D_MODEL
        gate = _sigmoid(_dot(u, w_ref[:, col:col + D_MODEL]))
        mixed = gate * up if mixed is None else mixed + gate * up
    o_ref[0] = x + _dot(mixed.astype(BF16), wo_ref[...])


def _merge(x, gain, w_all, ya, yb, yc, ym, wb_all, wo_all, layer):
    b, s, d = x.shape
    tm = TM_MERGE
    br = pl.BlockSpec((1, tm, BRANCH_WIDTH), lambda bi, i: (bi, i, 0))
    return pl.pallas_call(
        _merge_kernel,
        grid=(b, s // tm),
        in_specs=[
            pl.BlockSpec((1, tm, d), lambda bi, i: (bi, i, 0)),
            pl.BlockSpec((1, d), lambda bi, i: (0, 0)),
            pl.BlockSpec((None, d, IN_WIDTH), lambda bi, i: (layer, 0, 0),
                         pipeline_mode=pl.Buffered(1)),
            br, br, br, br,
            pl.BlockSpec((None, N_BRANCH, BRANCH_WIDTH, d), lambda bi, i: (layer, 0, 0, 0),
                         pipeline_mode=pl.Buffered(1)),
            pl.BlockSpec((None, d, d), lambda bi, i: (layer, 0, 0),
                         pipeline_mode=pl.Buffered(1)),
        ],
        out_specs=pl.BlockSpec((1, tm, d), lambda bi, i: (bi, i, 0)),
        out_shape=jax.ShapeDtypeStruct((b, s, d), F32),
        compiler_params=pltpu.CompilerParams(
            dimension_semantics=("parallel", "parallel"), vmem_limit_bytes=VMEM_LIMIT),
        name="merge_out",
    )(x, gain, w_all, ya, yb, yc, ym, wb_all, wo_all)


def _head_mean_matrix(width, head_dim):
    idx = jnp.arange(width) // head_dim
    return jnp.where(idx[:, None] == idx[None, :], 1.0 / head_dim, 0.0).astype(BF16)


def _block_diag(w):
    n, c, d = w.shape
    per_tile = MXU_TILE // c
    eye = jnp.eye(per_tile, dtype=w.dtype)
    grouped = w.reshape(n // per_tile, per_tile, c, d)
    return jnp.einsum('tncd,nm->tncmd', grouped, eye).reshape(n // per_tile, MXU_TILE, MXU_TILE)


def _layer(x, mem, layer, w_all, wmem_all, wb_all, wo_all, norm_gain, swa_q_gain, swa_k_gain,
           swa_sinks, conv_w, conv_b, lru_w_a, lru_b_a, lru_w_x, lru_b_x, lru_lambda,
           mem_norm_gain, mem_q_gain, mem_k_gain, consts):
    bdq, bdk, tri = consts
    row = lambda v: v.reshape(1, -1).astype(F32)
    mkt, mv = _memkv(mem, row(mem_norm_gain), wmem_all, row(mem_k_gain), layer)
    qg = row(jnp.tile(swa_q_gain, SWA_HEADS) * (SWA_HEAD_DIM ** -0.5))
    kg = row(jnp.tile(swa_k_gain, SWA_KV_HEADS))
    ya, yb, yc, ym = _branches(
        x, swa_sinks.reshape(1, SWA_HEADS).astype(F32), row(norm_gain), w_all, layer,
        qg, kg, bdq, bdk,
        conv_w.astype(F32), row(conv_b), _block_diag(lru_w_a).astype(BF16), row(lru_b_a),
        _block_diag(lru_w_x).astype(BF16), row(lru_b_x), row(lru_lambda),
        mkt, mv, row(mem_q_gain), tri)
    return _merge(x, row(norm_gain), w_all, ya, yb, yc, ym, wb_all, wo_all, layer)


def kernel(x, mem, norm_gain, w_in, swa_q_gain, swa_k_gain, swa_sinks, conv_w, conv_b, lru_w_a,
           lru_b_a, lru_w_x, lru_b_x, lru_lambda, mem_norm_gain, w_mem_kv, mem_q_gain,
           mem_k_gain, w_branch, w_out):
    depth = w_in.shape[0]
    consts = (
        _head_mean_matrix(MXU_TILE, SWA_HEAD_DIM),
        _head_mean_matrix(SWA_KV_HEADS * SWA_HEAD_DIM, SWA_HEAD_DIM),
        jnp.stack([jnp.tril(jnp.ones((TK_SB, TK_SB), F32), 0),
                   jnp.tril(jnp.ones((TK_SB, TK_SB), F32), -1)]).astype(BF16),
    )
    w_all, wmem_all, wb_all, wo_all = (w.astype(BF16) for w in (w_in, w_mem_kv, w_branch, w_out))
    for l in range(depth):
        x = _layer(x, mem, l, w_all, wmem_all, wb_all, wo_all, norm_gain[l], swa_q_gain[l],
                   swa_k_gain[l], swa_sinks[l], conv_w[l], conv_b[l], lru_w_a[l], lru_b_a[l],
                   lru_w_x[l], lru_b_x[l], lru_lambda[l], mem_norm_gain[l], mem_q_gain[l],
                   mem_k_gain[l], consts)
    return x
```

```python
import jax
import jax.numpy as jnp
from jax import lax
from jax.experimental import pallas as pl
from jax.experimental.pallas import tpu as pltpu

F32 = jnp.float32
BF16 = jnp.bfloat16

D_MODEL = 1024
BRANCH_WIDTH = 512
N_BRANCH = 4
SWA_HEAD_DIM = 64
SWA_HEADS = 8
SWA_KV_HEADS = 2
SWA_WINDOW = 128
SB_HEAD_DIM = 64
LRU_BLOCKS = 8
CONV_WIDTH = 4
LRU_C = 8.0
MEM_HEADS = 4
MEM_HEAD_DIM = 128
EPS = 1e-6

LANES = 128
SUBLANES = 8
MXU_TILE = 256
HEAD_PAIRS = BRANCH_WIDTH // LANES

_SIZES = (512, 128, 128, 512, 512, 512, 512, 512, 512, 512, 512, 512, 4096)
_OFFS = tuple(sum(_SIZES[:i]) for i in range(len(_SIZES)))
(OFF_AQ, OFF_AK, OFF_AV, OFF_AG, OFF_BQ, OFF_BK, OFF_BV, OFF_BG,
 OFF_CX, OFF_CG, OFF_MQ, OFF_MG, OFF_MERGE) = _OFFS
IN_WIDTH = sum(_SIZES)

TM_BRANCH = 512
TM_MERGE = 1024
TK_SB = 256
LRU_CHUNK = 128
FILL_COLS = 512

VMEM_LIMIT = 62 * 1024 * 1024

SB_EXIT_CARRY = 88.0


def _dot(a, b):
    return jnp.dot(a, b, preferred_element_type=F32)


def _dot_nt(a, b):
    return lax.dot_general(a, b, (((1,), (1,)), ((), ())), preferred_element_type=F32)


def _sigmoid(x):
    return 0.5 * jnp.tanh(0.5 * x) + 0.5


def _silu(x):
    h = 0.5 * x
    return h + h * jnp.tanh(h)


def _rms_scale(x):
    return lax.rsqrt(jnp.mean(x * x, axis=-1, keepdims=True) + EPS)


def _swa_branch(sink_ref, q, g, kcat, vcat, qg, kg, bdq, bdk, at_start, o_ref):
    tq, sub = q.shape[0], SWA_WINDOW
    nkeys = tq + sub
    q2 = (q * q).astype(BF16)
    msq = jnp.concatenate([_dot(q2[:, t * MXU_TILE:(t + 1) * MXU_TILE], bdq)
                           for t in range(q.shape[1] // MXU_TILE)], axis=1)
    qn = (q * lax.rsqrt(msq + EPS) * qg).astype(BF16)
    msk = _dot((kcat * kcat).astype(BF16), bdk)
    kn_t = (kcat * lax.rsqrt(msk + EPS) * kg).T
    zeros = jnp.zeros((SWA_HEAD_DIM, nkeys), F32)

    def k_variant(half, kv):
        blk = kn_t[kv * SWA_HEAD_DIM:(kv + 1) * SWA_HEAD_DIM]
        parts = [blk, zeros] if half == 0 else [zeros, blk]
        return jnp.concatenate(parts, axis=0).astype(BF16)

    kt = [[k_variant(half, kv) for kv in range(SWA_KV_HEADS)] for half in range(2)]
    v_same = vcat.astype(BF16)
    v_swap = pltpu.roll(vcat, SWA_HEAD_DIM, axis=1).astype(BF16)

    r = lax.broadcasted_iota(jnp.int32, (sub, 2 * sub), 0)
    j = lax.broadcasted_iota(jnp.int32, (sub, 2 * sub), 1)
    diff = r - (j - sub)
    band = (diff >= 0) & (diff < SWA_WINDOW)
    first = jnp.where(at_start, sub, 0)
    band_first = band & (j >= first)
    lane = lax.broadcasted_iota(jnp.int32, (sub, LANES), 1)
    row2 = lax.broadcasted_iota(jnp.int32, (2 * sub, 1), 0)

    for sb in range(tq // sub):
        rows = slice(sb * sub, (sb + 1) * sub)
        keys = slice(sb * sub, (sb + 2) * sub)
        valid = band_first if sb == 0 else band
        valid2 = jnp.concatenate([valid, valid], axis=0)
        outs = [None] * SWA_HEADS
        for kv in range(SWA_KV_HEADS):
            for half in range(2):
                pa, pb = 2 * kv, 2 * kv + 1
                ha, hb = 2 * pa + half, 2 * pb + half
                q2 = jnp.concatenate([qn[rows, pa * LANES:(pa + 1) * LANES],
                                      qn[rows, pb * LANES:(pb + 1) * LANES]], axis=0)
                sc = _dot(q2, kt[half][kv][:, keys])
                sc = jnp.where(valid2, sc, -jnp.inf)
                sink = jnp.where(row2 < sub, sink_ref[0, ha], sink_ref[0, hb])
                m = jnp.maximum(jnp.max(sc, axis=-1, keepdims=True), sink)
                pe = jnp.exp(sc - m)
                den = jnp.sum(pe, axis=-1, keepdims=True) + jnp.exp(sink - m)
                vv = v_same if half == kv else v_swap
                res = _dot(pe.astype(BF16), vv[keys]) * (1.0 / den)
                outs[ha], outs[hb] = res[:sub], res[sub:]
        blocks = [jnp.where(lane < SWA_HEAD_DIM, outs[2 * p], outs[2 * p + 1])
                  for p in range(HEAD_PAIRS)]
        y = jnp.concatenate(blocks, axis=1)
        o_ref[0, rows, :] = (y * _silu(g[rows])).astype(BF16)


def _lru_branch(x, g, cw_ref, cb_ref, wa_ref, ba_ref, wx_ref, bx_ref, lam_ref,
                xe_ref, h_ref, hs_ref, fill, never):
    tc = x.shape[0]
    halo = SUBLANES
    xe_ref[halo:halo + tc, :] = x
    xc = cb_ref[...] + cw_ref[CONV_WIDTH - 1:CONV_WIDTH, :] * x
    for tap in range(CONV_WIDTH - 1):
        back = CONV_WIDTH - 1 - tap
        xc = xc + cw_ref[tap:tap + 1, :] * xe_ref[halo - back:halo - back + tc, :]
    xe_ref[0:halo, :] = x[tc - halo:, :]

    xb = xc.astype(BF16)

    def gate(w_ref, b_ref):
        pre = [_dot(xb[:, t * MXU_TILE:(t + 1) * MXU_TILE], w_ref[t])
               for t in range(BRANCH_WIDTH // MXU_TILE)]
        return _sigmoid(jnp.concatenate(pre, axis=1) + b_ref[...])

    rg = gate(wa_ref, ba_ref)
    ig = gate(wx_ref, bx_ref)
    lam = lam_ref[...]
    softplus_neg_lam = jnp.maximum(-lam, 0.0) + jnp.log1p(jnp.exp(-jnp.abs(lam)))
    rate = -LRU_C * softplus_neg_lam

    groups = LRU_CHUNK // SUBLANES
    sub = lax.broadcasted_iota(jnp.int32, (groups, SUBLANES, BRANCH_WIDTH), 1)
    h = h_ref[...]
    for start in range(0, tc, LRU_CHUNK):
        rows = slice(start, start + LRU_CHUNK)
        a = jnp.exp(rg[rows] * rate)
        bv = jnp.sqrt(1.0 - a * a) * (ig[rows] * xc[rows])
        a3 = a.reshape(groups, SUBLANES, BRANCH_WIDTH)
        b3 = bv.reshape(groups, SUBLANES, BRANCH_WIDTH)
        shift = 1
        while shift < SUBLANES:
            keep = sub >= shift
            a_sh = pltpu.roll(a3, shift, axis=1)
            b_sh = pltpu.roll(b3, shift, axis=1)
            b3 = jnp.where(keep, a3 * b_sh + b3, b3)
            a3 = jnp.where(keep, a3 * a_sh, a3)
            shift *= 2
        for gi in range(groups):
            hg = a3[gi] * h + b3[gi]
            hs_ref[start + gi * SUBLANES:start + (gi + 1) * SUBLANES, :] = hg
            h = hg[SUBLANES - 1:SUBLANES, :]
        piece_row = fill()
        if piece_row is not None:
            rate = jnp.where(never, piece_row, rate)
    h_ref[...] = h
    return hs_ref[...] * _silu(g)


def _mem_branch(q, g, kt_ref, v_ref, qg):
    scale = MEM_HEAD_DIM ** -0.5
    outs = []
    for h in range(MEM_HEADS):
        sl = slice(h * MEM_HEAD_DIM, (h + 1) * MEM_HEAD_DIM)
        qh = q[:, sl]
        qn = (qh * _rms_scale(qh) * qg).astype(BF16)
        sc = _dot(qn, kt_ref[0, sl, :]) * scale
        m = jnp.max(sc, axis=-1, keepdims=True)
        pe = jnp.exp(sc - m)
        den = jnp.sum(pe, axis=-1, keepdims=True)
        outs.append(_dot(pe.astype(BF16), v_ref[0, :, sl]) * (1.0 / den))
    return jnp.concatenate(outs, axis=1) * _silu(g)


def _memkv_kernel(mem_ref, g_ref, w_ref, kg_ref, kt_ref, v_ref):
    m = mem_ref[0]
    u = (m * _rms_scale(m) * g_ref[...]).astype(BF16)
    kv = _dot(u, w_ref[...])
    for h in range(MEM_HEADS):
        kh = kv[:, h * MEM_HEAD_DIM:(h + 1) * MEM_HEAD_DIM]
        kn = kh * _rms_scale(kh) * kg_ref[...]
        kt_ref[0, h * MEM_HEAD_DIM:(h + 1) * MEM_HEAD_DIM, :] = kn.T.astype(BF16)
    v_ref[0] = kv[:, BRANCH_WIDTH:].astype(BF16)


def _memkv(mem, gain, w_all, kg, layer):
    b, mlen, d = mem.shape
    return pl.pallas_call(
        _memkv_kernel,
        grid=(b,),
        in_specs=[
            pl.BlockSpec((1, mlen, d), lambda bi: (bi, 0, 0)),
            pl.BlockSpec((1, d), lambda bi: (0, 0)),
            pl.BlockSpec((None, d, 2 * BRANCH_WIDTH), lambda bi: (layer, 0, 0)),
            pl.BlockSpec((1, MEM_HEAD_DIM), lambda bi: (0, 0)),
        ],
        out_specs=[pl.BlockSpec((1, BRANCH_WIDTH, mlen), lambda bi: (bi, 0, 0)),
                   pl.BlockSpec((1, mlen, BRANCH_WIDTH), lambda bi: (bi, 0, 0))],
        out_shape=[jax.ShapeDtypeStruct((b, BRANCH_WIDTH, mlen), BF16),
                   jax.ShapeDtypeStruct((b, mlen, BRANCH_WIDTH), BF16)],
        compiler_params=pltpu.CompilerParams(
            dimension_semantics=("parallel",), vmem_limit_bytes=VMEM_LIMIT),
        name="memkv",
    )(mem, gain, w_all, kg)


def _sb_tile(q_half, kj, vj, tri_incl, carries, causal, causal_bf16):
    rows, tk = q_half[0].shape[0], kj.shape[0]
    outs, new_carries = [], []
    for a in range(2):
        z = _dot_nt(q_half[a], kj)
        zb = z.astype(BF16)
        sp = jnp.maximum(zb, 0) + jnp.log(1 + jnp.exp(-jnp.abs(zb)))
        if causal_bf16 is not None:
            sp = sp * causal_bf16
        cum = _dot(sp, tri_incl)
        carry = carries[a]
        arg = z - (cum + jnp.concatenate([carry] * (tk // LANES), axis=1))
        if causal is not None:
            arg = jnp.where(causal, arg, -jnp.inf)
        outs.append(_dot(jnp.exp(arg).astype(BF16), vj))
        new_carries.append(carry + jnp.broadcast_to(cum[:, 0:1], (rows, LANES)))
    return outs, new_carries


def _q_halves(q):
    lane = lax.broadcasted_iota(jnp.int32, q.shape, 1)
    zero = jnp.zeros_like(q)
    return [jnp.where(lane < SB_HEAD_DIM, q, zero), jnp.where(lane >= SB_HEAD_DIM, q, zero)]


def _branch_kernel(sink_ref, x_ref, g_ref, w_ref, qg_ref, kg_ref, bdq_ref, bdk_ref,
                   cw_ref, cb_ref, wa_ref, ba_ref, wx_ref, bx_ref, lam_ref,
                   mkt_ref, mv_ref, mqg_ref, tri_ref,
                   ya_ref, yb_ref, yc_ref, ym_ref,
                   kvp_ref, xe_ref, h_ref, hs_ref, gs_ref, qs_ref, kall_ref, vall_ref,
                   acc_ref, car_ref):
    i = pl.program_id(1)
    tm, tb = TM_BRANCH, TK_SB
    nsub = tm // tb
    never = i < 0

    @pl.when(i == 0)
    def _():
        kvp_ref[...] = jnp.zeros_like(kvp_ref)
        xe_ref[0:SUBLANES, :] = jnp.zeros((SUBLANES, BRANCH_WIDTH), F32)
        h_ref[...] = jnp.zeros_like(h_ref)

    x = x_ref[0]
    u = (x * _rms_scale(x) * g_ref[...]).astype(BF16)

    def proj(off, width):
        return _dot(u, w_ref[:, off:off + width])

    row0 = pl.multiple_of(i * tm, tm)

    def b_piece(kind, col):
        def emit():
            blocks = range(col // LANES, (col + FILL_COLS) // LANES)
            if kind == "g":
                res = proj(OFF_BG + col, FILL_COLS)
                gs_ref[:, col:col + FILL_COLS] = _silu(res)
            elif kind == "q":
                res = proj(OFF_BQ + col, FILL_COLS) * (SB_HEAD_DIM ** -0.5)
                val = res.astype(BF16)
                for n, p in enumerate(blocks):
                    qs_ref[p] = val[:, n * LANES:(n + 1) * LANES]
            else:
                off, ref = (OFF_BK, kall_ref) if kind == "k" else (OFF_BV, vall_ref)
                res = proj(off + col, FILL_COLS)
                val = res.astype(BF16)
                for n, p in enumerate(blocks):
                    ref[p, pl.ds(row0, tm), :] = val[:, n * LANES:(n + 1) * LANES]
            return res[0:1, :]
        return emit

    pending = iter([b_piece(kind, col) for kind in "qkvg"
                    for col in range(0, BRANCH_WIDTH, FILL_COLS)])

    def fill():
        emit = next(pending, None)
        if emit is None:
            return None
        row = emit()
        return jnp.concatenate([row] * (BRANCH_WIDTH // FILL_COLS), axis=1)

    yc = _lru_branch(proj(OFF_CX, 512), proj(OFF_CG, 512), cw_ref, cb_ref, wa_ref, ba_ref,
                     wx_ref, bx_ref, lam_ref, xe_ref, h_ref, hs_ref, fill, never)
    yc_ref[0] = yc.astype(BF16)
    for emit in pending:
        emit()

    tri = tri_ref[0]
    strict = tri_ref[1]
    r = lax.broadcasted_iota(jnp.int32, (tb, tb), 0)
    c = lax.broadcasted_iota(jnp.int32, (tb, tb), 1)
    causal = c < r
    first = i * nsub

    def history_tile(p, jt):
        start = pl.multiple_of(jt * tb, tb)
        return kall_ref[p, pl.ds(start, tb), :], vall_ref[p, pl.ds(start, tb), :]

    def sb_pair(p):
        k0, v0 = history_tile(p, jnp.maximum(first - 1, 0))
        tiles = [(k0, jnp.where(first > 0, v0, jnp.zeros_like(v0)))]
        tiles += [history_tile(p, first + sb) for sb in range(nsub)]
        zeros = [jnp.zeros((tb, LANES), F32)] * 2
        for sb in range(nsub):
            qh = _q_halves(qs_ref[p, sb * tb:(sb + 1) * tb, :])
            out_d, car_d = _sb_tile(qh, *tiles[sb + 1], tri, zeros, causal, strict)
            out_l, car_l = _sb_tile(qh, *tiles[sb], tri, car_d, None, None)
            for a in range(2):
                acc_ref[p, sb, a] = out_d[a] + out_l[a]
                car_ref[p, sb, a] = car_l[a]

    for p in range(HEAD_PAIRS):
        sb_pair(p)

    kva = proj(OFF_AK, 2 * LANES)
    kvp = kvp_ref[...]
    kcat = jnp.concatenate([kvp[:, :LANES], kva[:, :LANES]], axis=0)
    vcat = jnp.concatenate([kvp[:, LANES:], kva[:, LANES:]], axis=0)
    kvp_ref[...] = kva[tm - SWA_WINDOW:, :]
    _swa_branch(sink_ref, proj(OFF_AQ, 512), proj(OFF_AG, 512), kcat, vcat, qg_ref[...],
                kg_ref[...], bdq_ref[...], bdk_ref[...], i == 0, ya_ref)

    ym = _mem_branch(proj(OFF_MQ, 512), proj(OFF_MG, 512), mkt_ref, mv_ref, mqg_ref[...])
    ym_ref[0] = ym.astype(BF16)

    def continue_left(p, sb):
        qh = _q_halves(qs_ref[p, pl.ds(pl.multiple_of(sb * tb, tb), tb), :])

        def not_done():
            return (jnp.min(car_ref[p, sb]) < SB_EXIT_CARRY).astype(jnp.int32)

        def cond(state):
            jt, live = state
            return jnp.logical_and(jt >= 0, live > 0)

        def body(state):
            jt, _ = state
            kj, vj = history_tile(p, jt)
            outs, cars = _sb_tile(qh, kj, vj, tri, [car_ref[p, sb, 0], car_ref[p, sb, 1]],
                                  None, None)
            for a in range(2):
                acc_ref[p, sb, a] += outs[a]
                car_ref[p, sb, a] = cars[a]
            return jt - 1, not_done()

        lax.while_loop(cond, body, (first + sb - 2, not_done()))

    @pl.when(jnp.min(car_ref[...]) < SB_EXIT_CARRY)
    def _():
        def over_pairs(p, unused):
            def over_subs(sb, unused2):
                continue_left(p, sb)
                return unused2
            return lax.fori_loop(0, nsub, over_subs, unused)
        lax.fori_loop(0, HEAD_PAIRS, over_pairs, 0)

    lane = lax.broadcasted_iota(jnp.int32, (tb, LANES), 1)
    for p in range(HEAD_PAIRS):
        for sb in range(nsub):
            rows = slice(sb * tb, (sb + 1) * tb)
            y = jnp.where(lane < SB_HEAD_DIM, acc_ref[p, sb, 0], acc_ref[p, sb, 1])
            yb_ref[0, rows, p * LANES:(p + 1) * LANES] = (
                y * gs_ref[rows, p * LANES:(p + 1) * LANES]).astype(BF16)


def _branches(x, sinks, gain, w_all, layer, qg, kg, bdq, bdk, cw, cb, wa, ba, wx, bx, lam, mkt,
              mv, mqg, tri):
    b, s, d = x.shape
    w = BRANCH_WIDTH
    mlen = mv.shape[1]
    tm, tb = TM_BRANCH, TK_SB
    const = lambda *shape: pl.BlockSpec(shape, lambda bi, i: (0,) * len(shape),
                                        pipeline_mode=pl.Buffered(1))
    row = pl.BlockSpec((1, tm, w), lambda bi, i: (bi, i, 0))
    out = jax.ShapeDtypeStruct((b, s, w), BF16)
    return pl.pallas_call(
        _branch_kernel,
        grid=(b, s // tm),
        in_specs=[
            pl.BlockSpec(memory_space=pltpu.SMEM),
            pl.BlockSpec((1, tm, d), lambda bi, i: (bi, i, 0)),
            const(1, d),
            pl.BlockSpec((None, d, OFF_MERGE), lambda bi, i: (layer, 0, 0),
                         pipeline_mode=pl.Buffered(1)),
            const(1, w), const(1, LANES), const(MXU_TILE, MXU_TILE), const(LANES, LANES),
            const(CONV_WIDTH, w), const(1, w), const(w // MXU_TILE, MXU_TILE, MXU_TILE),
            const(1, w), const(w // MXU_TILE, MXU_TILE, MXU_TILE),
            const(1, w), const(1, w),
            pl.BlockSpec((1, w, mlen), lambda bi, i: (bi, 0, 0)),
            pl.BlockSpec((1, mlen, w), lambda bi, i: (bi, 0, 0)),
            const(1, MEM_HEAD_DIM),
            const(2, tb, tb),
        ],
        out_specs=[row, row, row, row],
        out_shape=[out, out, out, out],
        scratch_shapes=[
            pltpu.VMEM((SWA_WINDOW, 2 * LANES), F32),
            pltpu.VMEM((tm + SUBLANES, w), F32),
            pltpu.VMEM((1, w), F32),
            pltpu.VMEM((tm, w), F32),
            pltpu.VMEM((tm, w), F32),
            pltpu.VMEM((HEAD_PAIRS, tm, LANES), BF16),
            pltpu.VMEM((HEAD_PAIRS, s, LANES), BF16),
            pltpu.VMEM((HEAD_PAIRS, s, LANES), BF16),
            pltpu.VMEM((HEAD_PAIRS, tm // tb, 2, tb, LANES), F32),
            pltpu.VMEM((HEAD_PAIRS, tm // tb, 2, tb, LANES), F32),
        ],
        compiler_params=pltpu.CompilerParams(
            dimension_semantics=("parallel", "arbitrary"), vmem_limit_bytes=VMEM_LIMIT),
        name="branches",
    )(sinks, x, gain, w_all, qg, kg, bdq, bdk, cw, cb, wa, ba, wx, bx, lam, mkt, mv, mqg, tri)


def _merge_kernel(x_ref, g_ref, w_ref, ya_ref, yb_ref, yc_ref, ym_ref, wb_ref, wo_ref, o_ref):
    x = x_ref[0]
    u = (x * _rms_scale(x) * g_ref[...]).astype(BF16)
    mixed = None
    for n, y_ref in enumerate((ya_ref, yb_ref, yc_ref, ym_ref)):
        up = _dot(y_ref[0], wb_ref[n])
        col = OFF_MERGE + n * D_MODEL
        gate = _sigmoid(_dot(u, w_ref[:, col:col + D_MODEL]))
        mixed = gate * up if mixed is None else mixed + gate * up
    o_ref[0] = x + _dot(mixed.astype(BF16), wo_ref[...])


def _merge(x, gain, w_all, ya, yb, yc, ym, wb_all, wo_all, layer):
    b, s, d = x.shape
    tm = TM_MERGE
    br = pl.BlockSpec((1, tm, BRANCH_WIDTH), lambda bi, i: (bi, i, 0))
    return pl.pallas_call(
        _merge_kernel,
        grid=(b, s // tm),
        in_specs=[
            pl.BlockSpec((1, tm, d), lambda bi, i: (bi, i, 0)),
            pl.BlockSpec((1, d), lambda bi, i: (0, 0)),
            pl.BlockSpec((None, d, IN_WIDTH), lambda bi, i: (layer, 0, 0),
                         pipeline_mode=pl.Buffered(1)),
            br, br, br, br,
            pl.BlockSpec((None, N_BRANCH, BRANCH_WIDTH, d), lambda bi, i: (layer, 0, 0, 0),
                         pipeline_mode=pl.Buffered(1)),
            pl.BlockSpec((None, d, d), lambda bi, i: (layer, 0, 0),
                         pipeline_mode=pl.Buffered(1)),
        ],
        out_specs=pl.BlockSpec((1, tm, d), lambda bi, i: (bi, i, 0)),
        out_shape=jax.ShapeDtypeStruct((b, s, d), F32),
        compiler_params=pltpu.CompilerParams(
            dimension_semantics=("parallel", "parallel"), vmem_limit_bytes=VMEM_LIMIT),
        name="merge_out",
    )(x, gain, w_all, ya, yb, yc, ym, wb_all, wo_all)


def _head_mean_matrix(width, head_dim):
    idx = jnp.arange(width) // head_dim
    return jnp.where(idx[:, None] == idx[None, :], 1.0 / head_dim, 0.0).astype(BF16)


def _block_diag(w):
    n, c, d = w.shape
    per_tile = MXU_TILE // c
    eye = jnp.eye(per_tile, dtype=w.dtype)
    grouped = w.reshape(n // per_tile, per_tile, c, d)
    return jnp.einsum('tncd,nm->tncmd', grouped, eye).reshape(n // per_tile, MXU_TILE, MXU_TILE)


def _layer(x, mem, layer, w_all, wmem_all, wb_all, wo_all, norm_gain, swa_q_gain, swa_k_gain,
           swa_sinks, conv_w, conv_b, lru_w_a, lru_b_a, lru_w_x, lru_b_x, lru_lambda,
           mem_norm_gain, mem_q_gain, mem_k_gain, consts):
    bdq, bdk, tri = consts
    row = lambda v: v.reshape(1, -1).astype(F32)
    mkt, mv = _memkv(mem, row(mem_norm_gain), wmem_all, row(mem_k_gain), layer)
    qg = row(jnp.tile(swa_q_gain, SWA_HEADS) * (SWA_HEAD_DIM ** -0.5))
    kg = row(jnp.tile(swa_k_gain, SWA_KV_HEADS))
    ya, yb, yc, ym = _branches(
        x, swa_sinks.reshape(1, SWA_HEADS).astype(F32), row(norm_gain), w_all, layer,
        qg, kg, bdq, bdk,
        conv_w.astype(F32), row(conv_b), _block_diag(lru_w_a).astype(BF16), row(lru_b_a),
        _block_diag(lru_w_x).astype(BF16), row(lru_b_x), row(lru_lambda),
        mkt, mv, row(mem_q_gain), tri)
    return _merge(x, row(norm_gain), w_all, ya, yb, yc, ym, wb_all, wo_all, layer)


def kernel(x, mem, norm_gain, w_in, swa_q_gain, swa_k_gain, swa_sinks, conv_w, conv_b, lru_w_a,
           lru_b_a, lru_w_x, lru_b_x, lru_lambda, mem_norm_gain, w_mem_kv, mem_q_gain,
           mem_k_gain, w_branch, w_out):
    depth = w_in.shape[0]
    consts = (
        _head_mean_matrix(MXU_TILE, SWA_HEAD_DIM),
        _head_mean_matrix(SWA_KV_HEADS * SWA_HEAD_DIM, SWA_HEAD_DIM),
        jnp.stack([jnp.tril(jnp.ones((TK_SB, TK_SB), F32), 0),
                   jnp.tril(jnp.ones((TK_SB, TK_SB), F32), -1)]).astype(BF16),
    )
    w_all, wmem_all, wb_all, wo_all = (w.astype(BF16) for w in (w_in, w_mem_kv, w_branch, w_out))
    for l in range(depth):
        x = _layer(x, mem, l, w_all, wmem_all, wb_all, wo_all, norm_gain[l], swa_q_gain[l],
                   swa_k_gain[l], swa_sinks[l], conv_w[l], conv_b[l], lru_w_a[l], lru_b_a[l],
                   lru_w_x[l], lru_b_x[l], lru_lambda[l], mem_norm_gain[l], mem_q_gain[l],
                   mem_k_gain[l], consts)
    return x
```

```python
import jax
import jax.numpy as jnp
from jax import lax
from jax.experimental import pallas as pl
from jax.experimental.pallas import tpu as pltpu

F32 = jnp.float32
BF16 = jnp.bfloat16

D_MODEL = 1024
BRANCH_WIDTH = 512
N_BRANCH = 4
SWA_HEAD_DIM = 64
SWA_HEADS = 8
SWA_KV_HEADS = 2
SWA_WINDOW = 128
SB_HEAD_DIM = 64
LRU_BLOCKS = 8
CONV_WIDTH = 4
LRU_C = 8.0
MEM_HEADS = 4
MEM_HEAD_DIM = 128
EPS = 1e-6

LANES = 128
SUBLANES = 8
MXU_TILE = 256
HEAD_PAIRS = BRANCH_WIDTH // LANES

_SIZES = (512, 128, 128, 512, 512, 512, 512, 512, 512, 512, 512, 512, 4096)
_OFFS = tuple(sum(_SIZES[:i]) for i in range(len(_SIZES)))
(OFF_AQ, OFF_AK, OFF_AV, OFF_AG, OFF_BQ, OFF_BK, OFF_BV, OFF_BG,
 OFF_CX, OFF_CG, OFF_MQ, OFF_MG, OFF_MERGE) = _OFFS
IN_WIDTH = sum(_SIZES)

TM_BRANCH = 512
TM_MERGE = 1024
MERGE_COLS = 512
TK_SB = 256
LRU_CHUNK = 128
FILL_COLS = 512

VMEM_LIMIT = 62 * 1024 * 1024

SB_EXIT_CARRY = 88.0


def _dot(a, b):
    return jnp.dot(a, b, preferred_element_type=F32)


def _dot_nt(a, b):
    return lax.dot_general(a, b, (((1,), (1,)), ((), ())), preferred_element_type=F32)


def _sigmoid(x):
    return 0.5 * jnp.tanh(0.5 * x) + 0.5


def _silu(x):
    h = 0.5 * x
    return h + h * jnp.tanh(h)


def _rms_scale(x):
    return lax.rsqrt(jnp.mean(x * x, axis=-1, keepdims=True) + EPS)


def _swa_branch(sink_ref, q, g, kcat, vcat, qg, kg, bdq, bdk, at_start, o_ref):
    tq, sub = q.shape[0], SWA_WINDOW
    nkeys = tq + sub
    q2 = (q * q).astype(BF16)
    msq = jnp.concatenate([_dot(q2[:, t * MXU_TILE:(t + 1) * MXU_TILE], bdq)
                           for t in range(q.shape[1] // MXU_TILE)], axis=1)
    qn = (q * lax.rsqrt(msq + EPS) * qg).astype(BF16)
    msk = _dot((kcat * kcat).astype(BF16), bdk)
    kn_t = (kcat * lax.rsqrt(msk + EPS) * kg).T
    zeros = jnp.zeros((SWA_HEAD_DIM, nkeys), F32)

    def k_variant(half, kv):
        blk = kn_t[kv * SWA_HEAD_DIM:(kv + 1) * SWA_HEAD_DIM]
        parts = [blk, zeros] if half == 0 else [zeros, blk]
        return jnp.concatenate(parts, axis=0).astype(BF16)

    kt = [[k_variant(half, kv) for kv in range(SWA_KV_HEADS)] for half in range(2)]
    v_same = vcat.astype(BF16)
    v_swap = pltpu.roll(vcat, SWA_HEAD_DIM, axis=1).astype(BF16)

    r = lax.broadcasted_iota(jnp.int32, (sub, 2 * sub), 0)
    j = lax.broadcasted_iota(jnp.int32, (sub, 2 * sub), 1)
    diff = r - (j - sub)
    band = (diff >= 0) & (diff < SWA_WINDOW)
    first = jnp.where(at_start, sub, 0)
    band_first = band & (j >= first)
    lane = lax.broadcasted_iota(jnp.int32, (sub, LANES), 1)
    row2 = lax.broadcasted_iota(jnp.int32, (2 * sub, 1), 0)

    for sb in range(tq // sub):
        rows = slice(sb * sub, (sb + 1) * sub)
        keys = slice(sb * sub, (sb + 2) * sub)
        valid = band_first if sb == 0 else band
        valid2 = jnp.concatenate([valid, valid], axis=0)
        outs = [None] * SWA_HEADS
        for kv in range(SWA_KV_HEADS):
            for half in range(2):
                pa, pb = 2 * kv, 2 * kv + 1
                ha, hb = 2 * pa + half, 2 * pb + half
                q2 = jnp.concatenate([qn[rows, pa * LANES:(pa + 1) * LANES],
                                      qn[rows, pb * LANES:(pb + 1) * LANES]], axis=0)
                sc = _dot(q2, kt[half][kv][:, keys])
                sc = jnp.where(valid2, sc, -jnp.inf)
                sink = jnp.where(row2 < sub, sink_ref[0, ha], sink_ref[0, hb])
                m = jnp.maximum(jnp.max(sc, axis=-1, keepdims=True), sink)
                pe = jnp.exp(sc - m)
                den = jnp.sum(pe, axis=-1, keepdims=True) + jnp.exp(sink - m)
                vv = v_same if half == kv else v_swap
                res = _dot(pe.astype(BF16), vv[keys]) * (1.0 / den)
                outs[ha], outs[hb] = res[:sub], res[sub:]
        blocks = [jnp.where(lane < SWA_HEAD_DIM, outs[2 * p], outs[2 * p + 1])
                  for p in range(HEAD_PAIRS)]
        y = jnp.concatenate(blocks, axis=1)
        o_ref[0, rows, :] = (y * _silu(g[rows])).astype(BF16)


def _lru_branch(x, g, cw_ref, cb_ref, wa_ref, ba_ref, wx_ref, bx_ref, lam_ref,
                xe_ref, h_ref, hs_ref, fill, never):
    tc = x.shape[0]
    halo = SUBLANES
    xe_ref[halo:halo + tc, :] = x
    xc = cb_ref[...] + cw_ref[CONV_WIDTH - 1:CONV_WIDTH, :] * x
    for tap in range(CONV_WIDTH - 1):
        back = CONV_WIDTH - 1 - tap
        xc = xc + cw_ref[tap:tap + 1, :] * xe_ref[halo - back:halo - back + tc, :]
    xe_ref[0:halo, :] = x[tc - halo:, :]

    xb = xc.astype(BF16)

    def gate(w_ref, b_ref):
        pre = [_dot(xb[:, t * MXU_TILE:(t + 1) * MXU_TILE], w_ref[t])
               for t in range(BRANCH_WIDTH // MXU_TILE)]
        return _sigmoid(jnp.concatenate(pre, axis=1) + b_ref[...])

    rg = gate(wa_ref, ba_ref)
    ig = gate(wx_ref, bx_ref)
    lam = lam_ref[...]
    softplus_neg_lam = jnp.maximum(-lam, 0.0) + jnp.log1p(jnp.exp(-jnp.abs(lam)))
    rate = -LRU_C * softplus_neg_lam

    groups = LRU_CHUNK // SUBLANES
    sub = lax.broadcasted_iota(jnp.int32, (groups, SUBLANES, BRANCH_WIDTH), 1)
    h = h_ref[...]
    for start in range(0, tc, LRU_CHUNK):
        rows = slice(start, start + LRU_CHUNK)
        a = jnp.exp(rg[rows] * rate)
        bv = jnp.sqrt(1.0 - a * a) * (ig[rows] * xc[rows])
        a3 = a.reshape(groups, SUBLANES, BRANCH_WIDTH)
        b3 = bv.reshape(groups, SUBLANES, BRANCH_WIDTH)
        shift = 1
        while shift < SUBLANES:
            keep = sub >= shift
            a_sh = pltpu.roll(a3, shift, axis=1)
            b_sh = pltpu.roll(b3, shift, axis=1)
            b3 = jnp.where(keep, a3 * b_sh + b3, b3)
            a3 = jnp.where(keep, a3 * a_sh, a3)
            shift *= 2
        for gi in range(groups):
            hg = a3[gi] * h + b3[gi]
            hs_ref[start + gi * SUBLANES:start + (gi + 1) * SUBLANES, :] = hg
            h = hg[SUBLANES - 1:SUBLANES, :]
        piece_row = fill()
        if piece_row is not None:
            rate = jnp.where(never, piece_row, rate)
    h_ref[...] = h
    return hs_ref[...] * _silu(g)


def _mem_branch(q, g, kt_ref, v_ref, qg):
    scale = MEM_HEAD_DIM ** -0.5
    outs = []
    for h in range(MEM_HEADS):
        sl = slice(h * MEM_HEAD_DIM, (h + 1) * MEM_HEAD_DIM)
        qh = q[:, sl]
        qn = (qh * _rms_scale(qh) * qg).astype(BF16)
        sc = _dot(qn, kt_ref[0, sl, :]) * scale
        m = jnp.max(sc, axis=-1, keepdims=True)
        pe = jnp.exp(sc - m)
        den = jnp.sum(pe, axis=-1, keepdims=True)
        outs.append(_dot(pe.astype(BF16), v_ref[0, :, sl]) * (1.0 / den))
    return jnp.concatenate(outs, axis=1) * _silu(g)


def _memkv_kernel(mem_ref, g_ref, w_ref, kg_ref, kt_ref, v_ref):
    m = mem_ref[0]
    u = (m * _rms_scale(m) * g_ref[...]).astype(BF16)
    kv = _dot(u, w_ref[...])
    for h in range(MEM_HEADS):
        kh = kv[:, h * MEM_HEAD_DIM:(h + 1) * MEM_HEAD_DIM]
        kn = kh * _rms_scale(kh) * kg_ref[...]
        kt_ref[0, h * MEM_HEAD_DIM:(h + 1) * MEM_HEAD_DIM, :] = kn.T.astype(BF16)
    v_ref[0] = kv[:, BRANCH_WIDTH:].astype(BF16)


def _memkv(mem, gain, w_all, kg, layer):
    b, mlen, d = mem.shape
    return pl.pallas_call(
        _memkv_kernel,
        grid=(b,),
        in_specs=[
            pl.BlockSpec((1, mlen, d), lambda bi: (bi, 0, 0)),
            pl.BlockSpec((1, d), lambda bi: (0, 0)),
            pl.BlockSpec((None, d, 2 * BRANCH_WIDTH), lambda bi: (layer, 0, 0)),
            pl.BlockSpec((1, MEM_HEAD_DIM), lambda bi: (0, 0)),
        ],
        out_specs=[pl.BlockSpec((1, BRANCH_WIDTH, mlen), lambda bi: (bi, 0, 0)),
                   pl.BlockSpec((1, mlen, BRANCH_WIDTH), lambda bi: (bi, 0, 0))],
        out_shape=[jax.ShapeDtypeStruct((b, BRANCH_WIDTH, mlen), BF16),
                   jax.ShapeDtypeStruct((b, mlen, BRANCH_WIDTH), BF16)],
        compiler_params=pltpu.CompilerParams(
            dimension_semantics=("parallel",), vmem_limit_bytes=VMEM_LIMIT),
        name="memkv",
    )(mem, gain, w_all, kg)


def _sb_tile(q_half, kj, vj, tri_incl, carries, causal, causal_bf16):
    rows, tk = q_half[0].shape[0], kj.shape[0]
    outs, new_carries = [], []
    for a in range(2):
        z = _dot_nt(q_half[a], kj)
        zb = z.astype(BF16)
        sp = jnp.maximum(zb, 0) + jnp.log(1 + jnp.exp(-jnp.abs(zb)))
        if causal_bf16 is not None:
            sp = sp * causal_bf16
        cum = _dot(sp, tri_incl)
        carry = carries[a]
        arg = z - (cum + jnp.concatenate([carry] * (tk // LANES), axis=1))
        if causal is not None:
            arg = jnp.where(causal, arg, -jnp.inf)
        outs.append(_dot(jnp.exp(arg).astype(BF16), vj))
        new_carries.append(carry + jnp.broadcast_to(cum[:, 0:1], (rows, LANES)))
    return outs, new_carries


def _q_halves(q):
    lane = lax.broadcasted_iota(jnp.int32, q.shape, 1)
    zero = jnp.zeros_like(q)
    return [jnp.where(lane < SB_HEAD_DIM, q, zero), jnp.where(lane >= SB_HEAD_DIM, q, zero)]


def _branch_kernel(sink_ref, x_ref, g_ref, w_ref, qg_ref, kg_ref, bdq_ref, bdk_ref,
                   cw_ref, cb_ref, wa_ref, ba_ref, wx_ref, bx_ref, lam_ref,
                   mkt_ref, mv_ref, mqg_ref, tri_ref,
                   ya_ref, yb_ref, yc_ref, ym_ref,
                   kvp_ref, xe_ref, h_ref, hs_ref, gs_ref, qs_ref, kall_ref, vall_ref,
                   acc_ref, car_ref):
    i = pl.program_id(1)
    tm, tb = TM_BRANCH, TK_SB
    nsub = tm // tb
    never = i < 0

    @pl.when(i == 0)
    def _():
        kvp_ref[...] = jnp.zeros_like(kvp_ref)
        xe_ref[0:SUBLANES, :] = jnp.zeros((SUBLANES, BRANCH_WIDTH), F32)
        h_ref[...] = jnp.zeros_like(h_ref)

    x = x_ref[0]
    u = (x * _rms_scale(x) * g_ref[...]).astype(BF16)

    def proj(off, width):
        return _dot(u, w_ref[:, off:off + width])

    row0 = pl.multiple_of(i * tm, tm)

    def b_piece(kind, col):
        def emit():
            blocks = range(col // LANES, (col + FILL_COLS) // LANES)
            if kind == "g":
                res = proj(OFF_BG + col, FILL_COLS)
                gs_ref[:, col:col + FILL_COLS] = _silu(res)
            elif kind == "q":
                res = proj(OFF_BQ + col, FILL_COLS) * (SB_HEAD_DIM ** -0.5)
                val = res.astype(BF16)
                for n, p in enumerate(blocks):
                    qs_ref[p] = val[:, n * LANES:(n + 1) * LANES]
            else:
                off, ref = (OFF_BK, kall_ref) if kind == "k" else (OFF_BV, vall_ref)
                res = proj(off + col, FILL_COLS)
                val = res.astype(BF16)
                for n, p in enumerate(blocks):
                    ref[p, pl.ds(row0, tm), :] = val[:, n * LANES:(n + 1) * LANES]
            return res[0:1, :]
        return emit

    pending = iter([b_piece(kind, col) for kind in "qkvg"
                    for col in range(0, BRANCH_WIDTH, FILL_COLS)])

    def fill():
        emit = next(pending, None)
        if emit is None:
            return None
        row = emit()
        return jnp.concatenate([row] * (BRANCH_WIDTH // FILL_COLS), axis=1)

    yc = _lru_branch(proj(OFF_CX, 512), proj(OFF_CG, 512), cw_ref, cb_ref, wa_ref, ba_ref,
                     wx_ref, bx_ref, lam_ref, xe_ref, h_ref, hs_ref, fill, never)
    yc_ref[0] = yc.astype(BF16)
    for emit in pending:
        emit()

    tri = tri_ref[0]
    strict = tri_ref[1]
    r = lax.broadcasted_iota(jnp.int32, (tb, tb), 0)
    c = lax.broadcasted_iota(jnp.int32, (tb, tb), 1)
    causal = c < r
    first = i * nsub

    def history_tile(p, jt):
        start = pl.multiple_of(jt * tb, tb)
        return kall_ref[p, pl.ds(start, tb), :], vall_ref[p, pl.ds(start, tb), :]

    def sb_pair(p):
        k0, v0 = history_tile(p, jnp.maximum(first - 1, 0))
        tiles = [(k0, jnp.where(first > 0, v0, jnp.zeros_like(v0)))]
        tiles += [history_tile(p, first + sb) for sb in range(nsub)]
        zeros = [jnp.zeros((tb, LANES), F32)] * 2
        for sb in range(nsub):
            qh = _q_halves(qs_ref[p, sb * tb:(sb + 1) * tb, :])
            out_d, car_d = _sb_tile(qh, *tiles[sb + 1], tri, zeros, causal, strict)
            out_l, car_l = _sb_tile(qh, *tiles[sb], tri, car_d, None, None)
            for a in range(2):
                acc_ref[p, sb, a] = out_d[a] + out_l[a]
                car_ref[p, sb, a] = car_l[a]

    for p in range(HEAD_PAIRS):
        sb_pair(p)

    kva = proj(OFF_AK, 2 * LANES)
    kvp = kvp_ref[...]
    kcat = jnp.concatenate([kvp[:, :LANES], kva[:, :LANES]], axis=0)
    vcat = jnp.concatenate([kvp[:, LANES:], kva[:, LANES:]], axis=0)
    kvp_ref[...] = kva[tm - SWA_WINDOW:, :]
    _swa_branch(sink_ref, proj(OFF_AQ, 512), proj(OFF_AG, 512), kcat, vcat, qg_ref[...],
                kg_ref[...], bdq_ref[...], bdk_ref[...], i == 0, ya_ref)

    ym = _mem_branch(proj(OFF_MQ, 512), proj(OFF_MG, 512), mkt_ref, mv_ref, mqg_ref[...])
    ym_ref[0] = ym.astype(BF16)

    def continue_left(p, sb):
        qh = _q_halves(qs_ref[p, pl.ds(pl.multiple_of(sb * tb, tb), tb), :])

        def not_done():
            return (jnp.min(car_ref[p, sb]) < SB_EXIT_CARRY).astype(jnp.int32)

        def cond(state):
            jt, live = state
            return jnp.logical_and(jt >= 0, live > 0)

        def body(state):
            jt, _ = state
            kj, vj = history_tile(p, jt)
            outs, cars = _sb_tile(qh, kj, vj, tri, [car_ref[p, sb, 0], car_ref[p, sb, 1]],
                                  None, None)
            for a in range(2):
                acc_ref[p, sb, a] += outs[a]
                car_ref[p, sb, a] = cars[a]
            return jt - 1, not_done()

        lax.while_loop(cond, body, (first + sb - 2, not_done()))

    @pl.when(jnp.min(car_ref[...]) < SB_EXIT_CARRY)
    def _():
        def over_pairs(p, unused):
            def over_subs(sb, unused2):
                continue_left(p, sb)
                return unused2
            return lax.fori_loop(0, nsub, over_subs, unused)
        lax.fori_loop(0, HEAD_PAIRS, over_pairs, 0)

    lane = lax.broadcasted_iota(jnp.int32, (tb, LANES), 1)
    for p in range(HEAD_PAIRS):
        for sb in range(nsub):
            rows = slice(sb * tb, (sb + 1) * tb)
            y = jnp.where(lane < SB_HEAD_DIM, acc_ref[p, sb, 0], acc_ref[p, sb, 1])
            yb_ref[0, rows, p * LANES:(p + 1) * LANES] = (
                y * gs_ref[rows, p * LANES:(p + 1) * LANES]).astype(BF16)


def _branches(x, sinks, gain, w_all, layer, qg, kg, bdq, bdk, cw, cb, wa, ba, wx, bx, lam, mkt,
              mv, mqg, tri):
    b, s, d = x.shape
    w = BRANCH_WIDTH
    mlen = mv.shape[1]
    tm, tb = TM_BRANCH, TK_SB
    const = lambda *shape: pl.BlockSpec(shape, lambda bi, i: (0,) * len(shape),
                                        pipeline_mode=pl.Buffered(1))
    row = pl.BlockSpec((1, tm, w), lambda bi, i: (bi, i, 0))
    out = jax.ShapeDtypeStruct((b, s, w), BF16)
    return pl.pallas_call(
        _branch_kernel,
        grid=(b, s // tm),
        in_specs=[
            pl.BlockSpec(memory_space=pltpu.SMEM),
            pl.BlockSpec((1, tm, d), lambda bi, i: (bi, i, 0)),
            const(1, d),
            pl.BlockSpec((None, d, OFF_MERGE), lambda bi, i: (layer, 0, 0),
                         pipeline_mode=pl.Buffered(1)),
            const(1, w), const(1, LANES), const(MXU_TILE, MXU_TILE), const(LANES, LANES),
            const(CONV_WIDTH, w), const(1, w), const(w // MXU_TILE, MXU_TILE, MXU_TILE),
            const(1, w), const(w // MXU_TILE, MXU_TILE, MXU_TILE),
            const(1, w), const(1, w),
            pl.BlockSpec((1, w, mlen), lambda bi, i: (bi, 0, 0)),
            pl.BlockSpec((1, mlen, w), lambda bi, i: (bi, 0, 0)),
            const(1, MEM_HEAD_DIM),
            const(2, tb, tb),
        ],
        out_specs=[row, row, row, row],
        out_shape=[out, out, out, out],
        scratch_shapes=[
            pltpu.VMEM((SWA_WINDOW, 2 * LANES), F32),
            pltpu.VMEM((tm + SUBLANES, w), F32),
            pltpu.VMEM((1, w), F32),
            pltpu.VMEM((tm, w), F32),
            pltpu.VMEM((tm, w), F32),
            pltpu.VMEM((HEAD_PAIRS, tm, LANES), BF16),
            pltpu.VMEM((HEAD_PAIRS, s, LANES), BF16),
            pltpu.VMEM((HEAD_PAIRS, s, LANES), BF16),
            pltpu.VMEM((HEAD_PAIRS, tm // tb, 2, tb, LANES), F32),
            pltpu.VMEM((HEAD_PAIRS, tm // tb, 2, tb, LANES), F32),
        ],
        compiler_params=pltpu.CompilerParams(
            dimension_semantics=("parallel", "arbitrary"), vmem_limit_bytes=VMEM_LIMIT),
        name="branches",
    )(sinks, x, gain, w_all, qg, kg, bdq, bdk, cw, cb, wa, ba, wx, bx, lam, mkt, mv, mqg, tri)


def _merge_kernel(x_ref, g_ref, w_ref, ya_ref, yb_ref, yc_ref, ym_ref, wb_ref, wo_ref, o_ref):
    x = x_ref[0]
    u = (x * _rms_scale(x) * g_ref[...]).astype(BF16)
    halves = []
    for c0 in range(0, D_MODEL, MERGE_COLS):
        mixed = None
        for n, y_ref in enumerate((ya_ref, yb_ref, yc_ref, ym_ref)):
            up = _dot(y_ref[0], wb_ref[n, :, c0:c0 + MERGE_COLS])
            col = OFF_MERGE + n * D_MODEL + c0
            gate = _sigmoid(_dot(u, w_ref[:, col:col + MERGE_COLS]))
            mixed = gate * up if mixed is None else mixed + gate * up
        halves.append(mixed.astype(BF16))
    o_ref[0] = x + _dot(jnp.concatenate(halves, axis=1), wo_ref[...])


def _merge(x, gain, w_all, ya, yb, yc, ym, wb_all, wo_all, layer):
    b, s, d = x.shape
    tm = TM_MERGE
    br = pl.BlockSpec((1, tm, BRANCH_WIDTH), lambda bi, i: (bi, i, 0))
    return pl.pallas_call(
        _merge_kernel,
        grid=(b, s // tm),
        in_specs=[
            pl.BlockSpec((1, tm, d), lambda bi, i: (bi, i, 0)),
            pl.BlockSpec((1, d), lambda bi, i: (0, 0)),
            pl.BlockSpec((None, d, IN_WIDTH), lambda bi, i: (layer, 0, 0),
                         pipeline_mode=pl.Buffered(1)),
            br, br, br, br,
            pl.BlockSpec((None, N_BRANCH, BRANCH_WIDTH, d), lambda bi, i: (layer, 0, 0, 0),
                         pipeline_mode=pl.Buffered(1)),
            pl.BlockSpec((None, d, d), lambda bi, i: (layer, 0, 0),
                         pipeline_mode=pl.Buffered(1)),
        ],
        out_specs=pl.BlockSpec((1, tm, d), lambda bi, i: (bi, i, 0)),
        out_shape=jax.ShapeDtypeStruct((b, s, d), F32),
        compiler_params=pltpu.CompilerParams(
            dimension_semantics=("parallel", "parallel"), vmem_limit_bytes=VMEM_LIMIT),
        name="merge_out",
    )(x, gain, w_all, ya, yb, yc, ym, wb_all, wo_all)


def _head_mean_matrix(width, head_dim):
    idx = jnp.arange(width) // head_dim
    return jnp.where(idx[:, None] == idx[None, :], 1.0 / head_dim, 0.0).astype(BF16)


def _block_diag(w):
    n, c, d = w.shape
    per_tile = MXU_TILE // c
    eye = jnp.eye(per_tile, dtype=w.dtype)
    grouped = w.reshape(n // per_tile, per_tile, c, d)
    return jnp.einsum('tncd,nm->tncmd', grouped, eye).reshape(n // per_tile, MXU_TILE, MXU_TILE)


def _layer(x, mem, layer, w_all, wmem_all, wb_all, wo_all, norm_gain, swa_q_gain, swa_k_gain,
           swa_sinks, conv_w, conv_b, lru_w_a, lru_b_a, lru_w_x, lru_b_x, lru_lambda,
           mem_norm_gain, mem_q_gain, mem_k_gain, consts):
    bdq, bdk, tri = consts
    row = lambda v: v.reshape(1, -1).astype(F32)
    mkt, mv = _memkv(mem, row(mem_norm_gain), wmem_all, row(mem_k_gain), layer)
    qg = row(jnp.tile(swa_q_gain, SWA_HEADS) * (SWA_HEAD_DIM ** -0.5))
    kg = row(jnp.tile(swa_k_gain, SWA_KV_HEADS))
    ya, yb, yc, ym = _branches(
        x, swa_sinks.reshape(1, SWA_HEADS).astype(F32), row(norm_gain), w_all, layer,
        qg, kg, bdq, bdk,
        conv_w.astype(F32), row(conv_b), _block_diag(lru_w_a).astype(BF16), row(lru_b_a),
        _block_diag(lru_w_x).astype(BF16), row(lru_b_x), row(lru_lambda),
        mkt, mv, row(mem_q_gain), tri)
    return _merge(x, row(norm_gain), w_all, ya, yb, yc, ym, wb_all, wo_all, layer)


def kernel(x, mem, norm_gain, w_in, swa_q_gain, swa_k_gain, swa_sinks, conv_w, conv_b, lru_w_a,
           lru_b_a, lru_w_x, lru_b_x, lru_lambda, mem_norm_gain, w_mem_kv, mem_q_gain,
           mem_k_gain, w_branch, w_out):
    depth = w_in.shape[0]
    consts = (
        _head_mean_matrix(MXU_TILE, SWA_HEAD_DIM),
        _head_mean_matrix(SWA_KV_HEADS * SWA_HEAD_DIM, SWA_HEAD_DIM),
        jnp.stack([jnp.tril(jnp.ones((TK_SB, TK_SB), F32), 0),
                   jnp.tril(jnp.ones((TK_SB, TK_SB), F32), -1)]).astype(BF16),
    )
    w_all, wmem_all, wb_all, wo_all = (w.astype(BF16) for w in (w_in, w_mem_kv, w_branch, w_out))
    for l in range(depth):
        x = _layer(x, mem, l, w_all, wmem_all, wb_all, wo_all, norm_gain[l], swa_q_gain[l],
                   swa_k_gain[l], swa_sinks[l], conv_w[l], conv_b[l], lru_w_a[l], lru_b_a[l],
                   lru_w_x[l], lru_b_x[l], lru_lambda[l], mem_norm_gain[l], mem_q_gain[l],
                   mem_k_gain[l], consts)
    return x
```
